```python
import jax, jax.numpy as jnp
from jax import lax
import numpy as np

D_MODEL = 1024
BATCH = 16
SEQ = 2048
DEPTH = 1

EPS = 1e-6
A_GROUPS = 8
A_GROUP_DIM = D_MODEL // A_GROUPS
A_WIDTH = A_GROUPS * A_GROUP_DIM
CHUNK = 128
MLA_HEADS = 8
QK_NOPE_DIM = 128
QK_ROPE_DIM = 64
QK_HEAD_DIM = QK_NOPE_DIM + QK_ROPE_DIM
V_HEAD_DIM = D_MODEL // MLA_HEADS
Q_LORA_RANK = 256
KV_LORA_RANK = 128
ROPE_THETA = 10000.0
Q_BLOCK = 128
D_FF = 2816
CONV_WIDTH = 3

IN_DIM = 2 * A_WIDTH + Q_LORA_RANK + KV_LORA_RANK + QK_ROPE_DIM + 2 * D_MODEL
SPLIT_U = A_WIDTH
SPLIT_V = SPLIT_U + A_WIDTH
SPLIT_CQ = SPLIT_V + Q_LORA_RANK
SPLIT_CKV = SPLIT_CQ + KV_LORA_RANK
SPLIT_KR = SPLIT_CKV + QK_ROPE_DIM
SPLIT_GA = SPLIT_KR + D_MODEL

kernel_name = "hybrid_gmlp_mla_convffn"


def rms_norm(x, g):
    xf = x.astype(jnp.float32)
    xf = xf * lax.rsqrt(jnp.mean(jnp.square(xf), axis=-1, keepdims=True) + EPS)
    return xf.astype(x.dtype) * g


def layer_norm(x, g, b):
    xf = x.astype(jnp.float32)
    mu = jnp.mean(xf, axis=-1, keepdims=True)
    var = jnp.mean(jnp.square(xf - mu), axis=-1, keepdims=True)
    return ((xf - mu) * lax.rsqrt(var + EPS)).astype(x.dtype) * g + b


def rope_cos_sin(positions):
    inv_freq = 1.0 / (ROPE_THETA ** (jnp.arange(0, QK_ROPE_DIM, 2, dtype=jnp.float32) / QK_ROPE_DIM))
    ang = positions.astype(jnp.float32)[..., None] * inv_freq
    return jnp.cos(ang), jnp.sin(ang)


def apply_rope(x, cos, sin):
    x1, x2 = jnp.split(x.astype(jnp.float32), 2, axis=-1)
    return jnp.concatenate([x1 * cos - x2 * sin, x1 * sin + x2 * cos], axis=-1).astype(x.dtype)


def chunked_spatial_gating(u, v, v_g, v_b, w_s, b_s):
    B, S, _ = v.shape
    n_chunks = S // CHUNK
    v = layer_norm(v, v_g, v_b)
    vc = v.reshape(B, n_chunks, CHUNK, A_GROUPS, A_GROUP_DIM)
    causal = jnp.tril(jnp.ones((CHUNK, CHUNK), dtype=bool))
    w = jnp.where(causal[None], w_s, 0.0).astype(vc.dtype)
    mixed = jnp.einsum('gts,bnsgc->bntgc', w, vc) + b_s.T[None, None, :, :, None]
    return u * mixed.reshape(B, S, A_WIDTH)


def latent_attention(c_q, c_kv, k_rope, cos, sin, q_norm_g, w_uq, kv_norm_g, w_ukv):
    B, S, _ = c_q.shape
    q = (rms_norm(c_q, q_norm_g) @ w_uq).reshape(B, S, MLA_HEADS, QK_HEAD_DIM)
    q_nope, q_rope = jnp.split(q, [QK_NOPE_DIM], axis=-1)
    q_rope = apply_rope(q_rope, cos[:, :, None, :], sin[:, :, None, :])
    kv = (rms_norm(c_kv, kv_norm_g) @ w_ukv).reshape(B, S, MLA_HEADS, QK_NOPE_DIM + V_HEAD_DIM)
    k_nope, v = jnp.split(kv, [QK_NOPE_DIM], axis=-1)
    k_rope = apply_rope(k_rope, cos, sin)
    scale = QK_HEAD_DIM ** -0.5
    n_blocks = S // Q_BLOCK
    qn_blocks = q_nope.reshape(B, n_blocks, Q_BLOCK, MLA_HEADS, QK_NOPE_DIM).transpose(1, 0, 2, 3, 4)
    qr_blocks = q_rope.reshape(B, n_blocks, Q_BLOCK, MLA_HEADS, QK_ROPE_DIM).transpose(1, 0, 2, 3, 4)
    key_pos = jnp.arange(S)

    def one_block(args):
        qn, qr, i = args
        s = jnp.einsum('bqhd,bkhd->bhqk', qn, k_nope) + jnp.einsum('bqhr,bkr->bhqk', qr, k_rope)
        s = s.astype(jnp.float32) * scale
        q_pos = i * Q_BLOCK + jnp.arange(Q_BLOCK)
        s = jnp.where(key_pos[None, :] <= q_pos[:, None], s, -jnp.inf)
        p = jax.nn.softmax(s, axis=-1).astype(v.dtype)
        return jnp.einsum('bhqk,bkhd->bqhd', p, v)

    out = lax.map(one_block, (qn_blocks, qr_blocks, jnp.arange(n_blocks)))
    return out.transpose(1, 0, 2, 3, 4).reshape(B, S, MLA_HEADS * V_HEAD_DIM)


def causal_depthwise_conv(x, w, b):
    S = x.shape[1]
    xp = jnp.pad(x, ((0, 0), (CONV_WIDTH - 1, 0), (0, 0)))
    return b + sum(w[k] * xp[:, k:k + S] for k in range(CONV_WIDTH))


def conv_gated_ffn(h, w_up, conv_w, conv_b, w_down):
    up = causal_depthwise_conv(h @ w_up, conv_w, conv_b)
    gate, val = jnp.split(up, 2, axis=-1)
    return (jax.nn.silu(gate) * val) @ w_down


def setup_inputs(seed: int = 0) -> dict:
    key = jax.random.key(seed)
    ks = jax.random.split(key, 20)
    f32 = jnp.float32

    def nrm(k, shape, fan_in):
        return jax.random.normal(k, shape, f32) * (fan_in ** -0.5)

    def gain(k, shape):
        return 1.0 + 0.02 * jax.random.normal(k, shape, f32)

    x = jax.random.normal(ks[0], (BATCH, SEQ, D_MODEL), f32)
    offset = jax.random.randint(ks[1], (BATCH, 1), 0, 1024, dtype=jnp.int32)
    positions = jnp.arange(SEQ, dtype=jnp.int32)[None, :] + offset
    return {
        "x": x,
        "positions": positions,
        "mix_norm": gain(ks[2], (DEPTH, D_MODEL)),
        "w_in": nrm(ks[3], (DEPTH, D_MODEL, IN_DIM), D_MODEL),
        "a_v_norm_g": gain(ks[4], (DEPTH, A_WIDTH)),
        "a_v_norm_b": 0.02 * jax.random.normal(ks[5], (DEPTH, A_WIDTH), f32),
        "a_spatial_w": nrm(ks[6], (DEPTH, A_GROUPS, CHUNK, CHUNK), CHUNK),
        "a_spatial_b": gain(ks[7], (DEPTH, A_GROUPS, CHUNK)),
        "q_a_norm": gain(ks[8], (DEPTH, Q_LORA_RANK)),
        "w_uq": nrm(ks[9], (DEPTH, Q_LORA_RANK, MLA_HEADS * QK_HEAD_DIM), Q_LORA_RANK),
        "kv_a_norm": gain(ks[10], (DEPTH, KV_LORA_RANK)),
        "w_ukv": nrm(ks[11], (DEPTH, KV_LORA_RANK, MLA_HEADS * (QK_NOPE_DIM + V_HEAD_DIM)), KV_LORA_RANK),
        "w_out": nrm(ks[12], (DEPTH, D_MODEL, D_MODEL), D_MODEL),
        "ffn_norm": gain(ks[13], (DEPTH, D_MODEL)),
        "w_up": nrm(ks[14], (DEPTH, D_MODEL, 2 * D_FF), D_MODEL),
        "conv_w": nrm(ks[15], (DEPTH, CONV_WIDTH, 2 * D_FF), CONV_WIDTH),
        "conv_b": 0.02 * jax.random.normal(ks[16], (DEPTH, 2 * D_FF), f32),
        "w_down": nrm(ks[17], (DEPTH, D_FF, D_MODEL), D_FF),
        "final_norm": gain(ks[18], (D_MODEL,)),
    }


def reference(x, positions, mix_norm, w_in, a_v_norm_g, a_v_norm_b, a_spatial_w, a_spatial_b,
              q_a_norm, w_uq, kv_a_norm, w_ukv, w_out, ffn_norm, w_up, conv_w, conv_b, w_down,
              final_norm):
    cos, sin = rope_cos_sin(positions)
    for l in range(DEPTH):
        h = rms_norm(x, mix_norm[l])
        z = h @ w_in[l]
        u_a, v_a, c_q, c_kv, k_rope, g_a, g_b = jnp.split(
            z, [SPLIT_U, SPLIT_V, SPLIT_CQ, SPLIT_CKV, SPLIT_KR, SPLIT_GA], axis=-1)
        y_a = chunked_spatial_gating(jax.nn.gelu(u_a), jax.nn.gelu(v_a), a_v_norm_g[l], a_v_norm_b[l],
                                     a_spatial_w[l], a_spatial_b[l])
        y_b = latent_attention(c_q, c_kv, k_rope, cos, sin, q_a_norm[l], w_uq[l],
                               kv_a_norm[l], w_ukv[l])
        merged = jax.nn.sigmoid(g_a) * y_a + jax.nn.sigmoid(g_b) * y_b
        x = x + merged @ w_out[l]
        x = x + conv_gated_ffn(rms_norm(x, ffn_norm[l]), w_up[l], conv_w[l], conv_b[l], w_down[l])
    return rms_norm(x, final_norm)
```

```python
import functools

import jax
import jax.numpy as jnp
from jax import lax
from jax.experimental import pallas as pl
from jax.experimental.pallas import tpu as pltpu

EPS = 1e-6
A_GROUPS = 8
A_GROUP_DIM = 128
CHUNK = 128
MLA_HEADS = 8
QK_NOPE_DIM = 128
QK_ROPE_DIM = 64
QK_HEAD_DIM = QK_NOPE_DIM + QK_ROPE_DIM
V_HEAD_DIM = 128
Q_LORA_RANK = 256
KV_LORA_RANK = 128
ROPE_THETA = 10000.0
CONV_WIDTH = 3

LANES = 128
SUBLANES = 8
VMEM_LIMIT_BYTES = 56 * 1024 * 1024

TOKEN_TILE = 256
Q_TILE = 256
FF_CHUNK = 256

F32 = jnp.float32
BF16 = jnp.bfloat16


def _rms(x, g):
    return x * lax.rsqrt(jnp.mean(x * x, axis=-1, keepdims=True) + EPS) * g


def _dot(a, b):
    return jnp.dot(a, b, preferred_element_type=F32)


def _mixer_in_kernel(x_ref, pos_ref, invf_ref, mixn_ref, win_ref, avg_ref, avb_ref, asw_ref, asb_ref,
                     qan_ref, wuq_ref, kvn_ref, wukv_ref,
                     ya_ref, gb_ref, qn_ref, qr_ref, kn_ref, kr_ref, v_ref, *, d_model, scale):
    tm = x_ref.shape[1]
    a_w = A_GROUPS * A_GROUP_DIM
    o_u, o_v = 0, a_w
    o_cq = 2 * a_w
    o_ckv = o_cq + Q_LORA_RANK
    o_kr = o_ckv + KV_LORA_RANK
    o_ga = o_kr + 2 * QK_ROPE_DIM
    o_gb = o_ga + d_model
    o_end = o_gb + d_model

    x = x_ref[0]
    h = _rms(x, mixn_ref[...]).astype(BF16)

    def proj(lo, hi):
        return _dot(h, win_ref[:, lo:hi])

    v_a = jax.nn.gelu(proj(o_v, o_cq))
    mu = jnp.mean(v_a, axis=-1, keepdims=True)
    var = jnp.mean(jnp.square(v_a - mu), axis=-1, keepdims=True)
    v_ln = (((v_a - mu) * lax.rsqrt(var + EPS)) * avg_ref[...] + avb_ref[...]).astype(BF16)
    pre = jax.nn.gelu(proj(o_u, o_v)) * jax.nn.sigmoid(proj(o_ga, o_gb))
    gb_ref[0] = jax.nn.sigmoid(proj(o_gb, o_end)).astype(BF16)

    t_idx = lax.broadcasted_iota(jnp.int32, (CHUNK, CHUNK), 0)
    s_idx = lax.broadcasted_iota(jnp.int32, (CHUNK, CHUNK), 1)
    causal = s_idx <= t_idx
    for g in range(A_GROUPS):
        w_g = jnp.where(causal, asw_ref[g], 0.0).astype(BF16)
        cols = slice(g * A_GROUP_DIM, (g + 1) * A_GROUP_DIM)
        for c in range(tm // CHUNK):
            rows = slice(c * CHUNK, (c + 1) * CHUNK)
            mixed = _dot(w_g, v_ln[rows, cols]) + asb_ref[:, cols]
            ya_ref[0, rows, cols] = (pre[rows, cols] * mixed).astype(BF16)

    ang = pos_ref[0].astype(F32) * invf_ref[...]
    cos = jnp.cos(ang)
    sin = jnp.sin(ang)
    lane = lax.broadcasted_iota(jnp.int32, (1, LANES), 1)
    first_half = (lane % QK_ROPE_DIM) < (QK_ROPE_DIM // 2)
    sin_x1 = jnp.where(first_half, -sin, 0.0)
    sin_x2 = jnp.where(first_half, 0.0, sin)
    half = QK_ROPE_DIM // 2

    def rope(blk):
        return (blk * cos + pltpu.roll(blk, LANES - half, 1) * sin_x1
                + pltpu.roll(blk, half, 1) * sin_x2)

    cqn = _rms(proj(o_cq, o_ckv), qan_ref[...]).astype(BF16)
    q = _dot(cqn, wuq_ref[...])
    n_nope = MLA_HEADS * QK_NOPE_DIM
    qn_ref[0] = (q[:, :n_nope] * scale).astype(BF16)
    for p in range(MLA_HEADS // 2):
        cols = slice(p * LANES, (p + 1) * LANES)
        qr_ref[0, :, cols] = (rope(q[:, n_nope + p * LANES:n_nope + (p + 1) * LANES]) * scale).astype(BF16)

    ckvn = _rms(proj(o_ckv, o_kr), kvn_ref[...]).astype(BF16)
    kv = _dot(ckvn, wukv_ref[...])
    kn_ref[0] = kv[:, :n_nope].astype(BF16)
    v_ref[0] = kv[:, n_nope:].astype(BF16)
    k_rot = rope(proj(o_kr, o_ga))
    kr_ref[0, 0] = jnp.where(lane < QK_ROPE_DIM, k_rot, 0.0).astype(BF16)
    kr_ref[0, 1] = jnp.where(lane >= QK_ROPE_DIM, k_rot, 0.0).astype(BF16)


def _attention_kernel(qn_ref, qr_ref, kn_ref, kr_ref, v_ref, ya_ref, gb_ref, o_ref, k_scr, vt_scr, *, tq):
    seq = qn_ref.shape[1]
    k_scr[...] = jnp.concatenate([kn_ref[0], kr_ref[0, 0]], axis=1)
    vt_scr[...] = v_ref[0].astype(F32).T.astype(BF16)
    nt = (((1,), (1,)), ((), ()))
    key_i = lax.broadcasted_iota(jnp.int32, (tq, tq), 0)
    qry_i = lax.broadcasted_iota(jnp.int32, (tq, tq), 1)
    visible = key_i <= qry_i
    for qi in range(seq // tq):
        r0 = qi * tq
        rows = slice(r0, r0 + tq)
        q = jnp.concatenate([qn_ref[0, rows, :], qr_ref[0, rows, :]], axis=1)
        s_d = lax.dot_general(k_scr[rows, :], q, nt, preferred_element_type=F32)
        s_d = jnp.where(visible, s_d, -jnp.inf)
        m = jnp.max(s_d, axis=0, keepdims=True)
        if r0 > 0:
            s_o = lax.dot_general(k_scr[0:r0, :], q, nt, preferred_element_type=F32)
            m = jnp.maximum(m, jnp.max(s_o, axis=0, keepdims=True))
            p_o = jnp.exp(s_o - m)
            l = jnp.sum(p_o, axis=0, keepdims=True)
            acc = _dot(vt_scr[:, 0:r0], p_o.astype(BF16))
        p_d = jnp.exp(s_d - m)
        if r0 > 0:
            l = l + jnp.sum(p_d, axis=0, keepdims=True)
            acc = acc + _dot(vt_scr[:, rows], p_d.astype(BF16))
        else:
            l = jnp.sum(p_d, axis=0, keepdims=True)
            acc = _dot(vt_scr[:, rows], p_d.astype(BF16))
        y_b = (acc / l).T
        merged = ya_ref[0, rows, :].astype(F32) + gb_ref[0, rows, :].astype(F32) * y_b
        o_ref[0, rows, :] = merged.astype(BF16)


def _ffn_kernel(x_ref, mg_ref, wout_ref, ffn_ref, wup_ref, cw_ref, cb_ref, wdown_ref, fin_ref,
                o_ref, up_scr, act_scr, *, d_ff, apply_final):
    tm = x_ref.shape[1]
    halo = SUBLANES
    j = pl.program_id(1)

    @pl.when(j == 0)
    def _():
        up_scr[0:halo, :] = jnp.zeros((halo, up_scr.shape[1]), F32)

    @pl.when(j > 0)
    def _():
        up_scr[0:halo, :] = up_scr[tm:tm + halo, :]

    x1 = x_ref[0] + _dot(mg_ref[0], wout_ref[...])
    o_ref[0] = x1
    up_scr[halo:halo + tm, :] = _dot(_rms(x1, ffn_ref[...]).astype(BF16), wup_ref[...])

    def conv(lo):
        cols = slice(lo, lo + FF_CHUNK)
        taps = 0.0
        for k in range(CONV_WIDTH):
            r0 = halo - (CONV_WIDTH - 1) + k
            taps = taps + cw_ref[k:k + 1, cols] * up_scr[r0:r0 + tm, cols]
        return cb_ref[:, cols] + taps

    for c in range(0, d_ff, FF_CHUNK):
        act_scr[:, c:c + FF_CHUNK] = (jax.nn.silu(conv(c)) * conv(d_ff + c)).astype(BF16)

    x2 = o_ref[0] + _dot(act_scr[...], wdown_ref[...])
    o_ref[0] = _rms(x2, fin_ref[...]) if apply_final else x2


def _const_spec(shape):
    nd = len(shape)
    return pl.BlockSpec(shape, lambda *_: (0,) * nd, pipeline_mode=pl.Buffered(1))


def _params(n_axes):
    return pltpu.CompilerParams(dimension_semantics=("arbitrary",) * n_axes,
                                vmem_limit_bytes=VMEM_LIMIT_BYTES)


def _layer(x, pos3, inv_freq, mix_norm, w_in, a_v_norm_g, a_v_norm_b, a_spatial_w, a_spatial_b,
           q_a_norm, w_uq, kv_a_norm, w_ukv, w_out, ffn_norm, w_up, conv_w, conv_b, w_down,
           final_norm, apply_final):
    batch, seq, d_model = x.shape
    heads = MLA_HEADS
    a_w = A_GROUPS * A_GROUP_DIM
    d_ff = w_down.shape[0]
    assert d_model == a_w == heads * V_HEAD_DIM
    assert seq % TOKEN_TILE == 0 and TOKEN_TILE % CHUNK == 0 and seq % Q_TILE == 0
    assert d_ff % FF_CHUNK == 0
    tm = TOKEN_TILE
    n_tiles = seq // tm

    s_ckv = 2 * a_w + Q_LORA_RANK
    s_kr = s_ckv + KV_LORA_RANK
    s_ga = s_kr + QK_ROPE_DIM
    win_p = jnp.concatenate([w_in[:, :s_ga], w_in[:, s_kr:s_ga], w_in[:, s_ga:]], axis=1).astype(BF16)
    wuq3 = w_uq.reshape(Q_LORA_RANK, heads, QK_HEAD_DIM)
    wuq_p = jnp.concatenate([wuq3[:, :, :QK_NOPE_DIM].reshape(Q_LORA_RANK, heads * QK_NOPE_DIM),
                             wuq3[:, :, QK_NOPE_DIM:].reshape(Q_LORA_RANK, heads * QK_ROPE_DIM)],
                            axis=1).astype(BF16)
    wukv3 = w_ukv.reshape(KV_LORA_RANK, heads, QK_NOPE_DIM + V_HEAD_DIM)
    wukv_p = jnp.concatenate([wukv3[:, :, :QK_NOPE_DIM].reshape(KV_LORA_RANK, heads * QK_NOPE_DIM),
                              wukv3[:, :, QK_NOPE_DIM:].reshape(KV_LORA_RANK, heads * V_HEAD_DIM)],
                             axis=1).astype(BF16)
    asb_full = jnp.repeat(a_spatial_b.T, A_GROUP_DIM, axis=1)
    row = lambda a: a.reshape(1, -1)

    tok = lambda w: pl.BlockSpec((1, tm, w), lambda b, i: (b, i, 0))
    tok_out = lambda w: jax.ShapeDtypeStruct((batch, seq, w), BF16)
    n_rope_blk = (heads // 2) * LANES

    ya, gb, qn, qr, kn, kr, v = pl.pallas_call(
        functools.partial(_mixer_in_kernel, d_model=d_model, scale=QK_HEAD_DIM ** -0.5),
        grid=(batch, n_tiles),
        in_specs=[tok(d_model),
                  pl.BlockSpec((1, tm, 1), lambda b, i: (b, i, 0)),
                  _const_spec((1, LANES)), _const_spec((1, d_model)), _const_spec(win_p.shape),
                  _const_spec((1, a_w)), _const_spec((1, a_w)), _const_spec(a_spatial_w.shape),
                  _const_spec(asb_full.shape), _const_spec((1, Q_LORA_RANK)), _const_spec(wuq_p.shape),
                  _const_spec((1, KV_LORA_RANK)), _const_spec(wukv_p.shape)],
        out_specs=[tok(a_w), tok(d_model), tok(heads * QK_NOPE_DIM), tok(n_rope_blk),
                   tok(heads * QK_NOPE_DIM),
                   pl.BlockSpec((1, 2, tm, LANES), lambda b, i: (b, 0, i, 0)),
                   tok(heads * V_HEAD_DIM)],
        out_shape=[tok_out(a_w), tok_out(d_model), tok_out(heads * QK_NOPE_DIM), tok_out(n_rope_blk),
                   tok_out(heads * QK_NOPE_DIM),
                   jax.ShapeDtypeStruct((batch, 2, seq, LANES), BF16),
                   tok_out(heads * V_HEAD_DIM)],
        compiler_params=_params(2),
        name="mixer_in",
    )(x, pos3, inv_freq, row(mix_norm), win_p, row(a_v_norm_g), row(a_v_norm_b), a_spatial_w, asb_full,
      row(q_a_norm), wuq_p, row(kv_a_norm), wukv_p)

    head_blk = pl.BlockSpec((1, seq, LANES), lambda b, h: (b, 0, h))
    merged = pl.pallas_call(
        functools.partial(_attention_kernel, tq=Q_TILE),
        grid=(batch, heads),
        in_specs=[head_blk,
                  pl.BlockSpec((1, seq, LANES), lambda b, h: (b, 0, h // 2)),
                  head_blk,
                  pl.BlockSpec((1, 1, seq, LANES), lambda b, h: (b, h % 2, 0, 0)),
                  head_blk, head_blk, head_blk],
        out_specs=head_blk,
        out_shape=jax.ShapeDtypeStruct((batch, seq, d_model), BF16),
        scratch_shapes=[pltpu.VMEM((seq, 2 * LANES), BF16), pltpu.VMEM((V_HEAD_DIM, seq), BF16)],
        compiler_params=_params(2),
        name="attention",
    )(qn, qr, kn, kr, v, ya, gb)

    out = pl.pallas_call(
        functools.partial(_ffn_kernel, d_ff=d_ff, apply_final=apply_final),
        grid=(batch, n_tiles),
        in_specs=[tok(d_model), tok(d_model),
                  _const_spec((d_model, d_model)), _const_spec((1, d_model)),
                  _const_spec((d_model, 2 * d_ff)), _const_spec((CONV_WIDTH, 2 * d_ff)),
                  _const_spec((1, 2 * d_ff)), _const_spec((d_ff, d_model)), _const_spec((1, d_model))],
        out_specs=tok(d_model),
        out_shape=jax.ShapeDtypeStruct((batch, seq, d_model), F32),
        scratch_shapes=[pltpu.VMEM((SUBLANES + tm, 2 * d_ff), F32), pltpu.VMEM((tm, d_ff), BF16)],
        compiler_params=_params(2),
        name="ffn",
    )(x, merged, w_out.astype(BF16), row(ffn_norm), w_up.astype(BF16), conv_w, row(conv_b),
      w_down.astype(BF16), row(final_norm))
    return out


def kernel(x, positions, mix_norm, w_in, a_v_norm_g, a_v_norm_b, a_spatial_w, a_spatial_b, q_a_norm, w_uq,
           kv_a_norm, w_ukv, w_out, ffn_norm, w_up, conv_w, conv_b, w_down, final_norm):
    depth = w_in.shape[0]
    batch, seq, _ = x.shape
    pos3 = positions.reshape(batch, seq, 1)
    inv_freq = 1.0 / (ROPE_THETA ** (jnp.arange(0, QK_ROPE_DIM, 2, dtype=F32) / QK_ROPE_DIM))
    inv_freq = jnp.tile(inv_freq, LANES // (QK_ROPE_DIM // 2)).reshape(1, LANES)
    for l in range(depth):
        x = _layer(x, pos3, inv_freq, mix_norm[l], w_in[l], a_v_norm_g[l], a_v_norm_b[l], a_spatial_w[l],
                   a_spatial_b[l], q_a_norm[l], w_uq[l], kv_a_norm[l], w_ukv[l], w_out[l], ffn_norm[l],
                   w_up[l], conv_w[l], conv_b[l], w_down[l], final_norm, apply_final=(l == depth - 1))
    return x
```

```python
import functools

import jax
import jax.numpy as jnp
from jax import lax
from jax.experimental import pallas as pl
from jax.experimental.pallas import tpu as pltpu

EPS = 1e-6
A_GROUPS = 8
A_GROUP_DIM = 128
CHUNK = 128
MLA_HEADS = 8
QK_NOPE_DIM = 128
QK_ROPE_DIM = 64
QK_HEAD_DIM = QK_NOPE_DIM + QK_ROPE_DIM
V_HEAD_DIM = 128
Q_LORA_RANK = 256
KV_LORA_RANK = 128
ROPE_THETA = 10000.0
CONV_WIDTH = 3

LANES = 128
SUBLANES = 8
VMEM_LIMIT_BYTES = 56 * 1024 * 1024

TOKEN_TILE = 512
Q_TILE = 512
FF_CHUNK = 256

F32 = jnp.float32
BF16 = jnp.bfloat16


def _rms(x, g):
    return x * lax.rsqrt(jnp.mean(x * x, axis=-1, keepdims=True) + EPS) * g


def _dot(a, b):
    return jnp.dot(a, b, preferred_element_type=F32)


def _mixer_in_kernel(x_ref, pos_ref, invf_ref, mixn_ref, win_ref, avg_ref, avb_ref, asw_ref, asb_ref,
                     qan_ref, wuq_ref, kvn_ref, wukv_ref,
                     ya_ref, gb_ref, qn_ref, qr_ref, kn_ref, kr_ref, v_ref, *, d_model, scale):
    tm = x_ref.shape[1]
    a_w = A_GROUPS * A_GROUP_DIM
    o_u, o_v = 0, a_w
    o_cq = 2 * a_w
    o_ckv = o_cq + Q_LORA_RANK
    o_kr = o_ckv + KV_LORA_RANK
    o_ga = o_kr + 2 * QK_ROPE_DIM
    o_gb = o_ga + d_model
    o_end = o_gb + d_model

    x = x_ref[0]
    h = _rms(x, mixn_ref[...]).astype(BF16)

    def proj(lo, hi):
        return _dot(h, win_ref[:, lo:hi])

    v_a = jax.nn.gelu(proj(o_v, o_cq))
    mu = jnp.mean(v_a, axis=-1, keepdims=True)
    var = jnp.mean(jnp.square(v_a - mu), axis=-1, keepdims=True)
    v_ln = (((v_a - mu) * lax.rsqrt(var + EPS)) * avg_ref[...] + avb_ref[...]).astype(BF16)
    pre = jax.nn.gelu(proj(o_u, o_v)) * jax.nn.sigmoid(proj(o_ga, o_gb))
    gb_ref[0] = jax.nn.sigmoid(proj(o_gb, o_end)).astype(BF16)

    t_idx = lax.broadcasted_iota(jnp.int32, (CHUNK, CHUNK), 0)
    s_idx = lax.broadcasted_iota(jnp.int32, (CHUNK, CHUNK), 1)
    causal = s_idx <= t_idx
    for g in range(A_GROUPS):
        w_g = jnp.where(causal, asw_ref[g], 0.0).astype(BF16)
        cols = slice(g * A_GROUP_DIM, (g + 1) * A_GROUP_DIM)
        for c in range(tm // CHUNK):
            rows = slice(c * CHUNK, (c + 1) * CHUNK)
            mixed = _dot(w_g, v_ln[rows, cols]) + asb_ref[:, cols]
            ya_ref[0, rows, cols] = (pre[rows, cols] * mixed).astype(BF16)

    ang = pos_ref[0].astype(F32) * invf_ref[...]
    cos = jnp.cos(ang)
    sin = jnp.sin(ang)
    lane = lax.broadcasted_iota(jnp.int32, (1, LANES), 1)
    first_half = (lane % QK_ROPE_DIM) < (QK_ROPE_DIM // 2)
    sin_x1 = jnp.where(first_half, -sin, 0.0)
    sin_x2 = jnp.where(first_half, 0.0, sin)
    half = QK_ROPE_DIM // 2

    def rope(blk):
        return (blk * cos + pltpu.roll(blk, LANES - half, 1) * sin_x1
                + pltpu.roll(blk, half, 1) * sin_x2)

    cqn = _rms(proj(o_cq, o_ckv), qan_ref[...]).astype(BF16)
    q = _dot(cqn, wuq_ref[...])
    n_nope = MLA_HEADS * QK_NOPE_DIM
    qn_ref[0] = (q[:, :n_nope] * scale).astype(BF16)
    for p in range(MLA_HEADS // 2):
        cols = slice(p * LANES, (p + 1) * LANES)
        qr_ref[0, :, cols] = (rope(q[:, n_nope + p * LANES:n_nope + (p + 1) * LANES]) * scale).astype(BF16)

    ckvn = _rms(proj(o_ckv, o_kr), kvn_ref[...]).astype(BF16)
    kv = _dot(ckvn, wukv_ref[...])
    kn_ref[0] = kv[:, :n_nope].astype(BF16)
    v_ref[0] = kv[:, n_nope:].astype(BF16)
    k_rot = rope(proj(o_kr, o_ga))
    kr_ref[0, 0] = jnp.where(lane < QK_ROPE_DIM, k_rot, 0.0).astype(BF16)
    kr_ref[0, 1] = jnp.where(lane >= QK_ROPE_DIM, k_rot, 0.0).astype(BF16)


def _attention_kernel(qn_ref, qr_ref, kn_ref, kr_ref, v_ref, ya_ref, gb_ref, o_ref, k_scr, vt_scr, *, tq):
    seq = qn_ref.shape[1]
    k_scr[...] = jnp.concatenate([kn_ref[0], kr_ref[0, 0]], axis=1)
    vt_scr[...] = v_ref[0].astype(F32).T.astype(BF16)
    nt = (((1,), (1,)), ((), ()))
    key_i = lax.broadcasted_iota(jnp.int32, (tq, tq), 0)
    qry_i = lax.broadcasted_iota(jnp.int32, (tq, tq), 1)
    visible = key_i <= qry_i
    for qi in range(seq // tq):
        r0 = qi * tq
        rows = slice(r0, r0 + tq)
        q = jnp.concatenate([qn_ref[0, rows, :], qr_ref[0, rows, :]], axis=1)
        s_d = lax.dot_general(k_scr[rows, :], q, nt, preferred_element_type=F32)
        s_d = jnp.where(visible, s_d, -jnp.inf)
        m = jnp.max(s_d, axis=0, keepdims=True)
        if r0 > 0:
            s_o = lax.dot_general(k_scr[0:r0, :], q, nt, preferred_element_type=F32)
            m = jnp.maximum(m, jnp.max(s_o, axis=0, keepdims=True))
            p_o = jnp.exp(s_o - m)
            l = jnp.sum(p_o, axis=0, keepdims=True)
            acc = _dot(vt_scr[:, 0:r0], p_o.astype(BF16))
        p_d = jnp.exp(s_d - m)
        if r0 > 0:
            l = l + jnp.sum(p_d, axis=0, keepdims=True)
            acc = acc + _dot(vt_scr[:, rows], p_d.astype(BF16))
        else:
            l = jnp.sum(p_d, axis=0, keepdims=True)
            acc = _dot(vt_scr[:, rows], p_d.astype(BF16))
        y_b = (acc / l).T
        merged = ya_ref[0, rows, :].astype(F32) + gb_ref[0, rows, :].astype(F32) * y_b
        o_ref[0, rows, :] = merged.astype(BF16)


def _ffn_kernel(x_ref, mg_ref, wout_ref, ffn_ref, wup_ref, cw_ref, cb_ref, wdown_ref, fin_ref,
                o_ref, up_scr, act_scr, *, d_ff, apply_final):
    tm = x_ref.shape[1]
    halo = SUBLANES
    j = pl.program_id(1)

    @pl.when(j == 0)
    def _():
        up_scr[0:halo, :] = jnp.zeros((halo, up_scr.shape[1]), F32)

    @pl.when(j > 0)
    def _():
        up_scr[0:halo, :] = up_scr[tm:tm + halo, :]

    x1 = x_ref[0] + _dot(mg_ref[0], wout_ref[...])
    o_ref[0] = x1
    up_scr[halo:halo + tm, :] = _dot(_rms(x1, ffn_ref[...]).astype(BF16), wup_ref[...])

    def conv(lo):
        cols = slice(lo, lo + FF_CHUNK)
        taps = 0.0
        for k in range(CONV_WIDTH):
            r0 = halo - (CONV_WIDTH - 1) + k
            taps = taps + cw_ref[k:k + 1, cols] * up_scr[r0:r0 + tm, cols]
        return cb_ref[:, cols] + taps

    for c in range(0, d_ff, FF_CHUNK):
        act_scr[:, c:c + FF_CHUNK] = (jax.nn.silu(conv(c)) * conv(d_ff + c)).astype(BF16)

    x2 = o_ref[0] + _dot(act_scr[...], wdown_ref[...])
    o_ref[0] = _rms(x2, fin_ref[...]) if apply_final else x2


def _const_spec(shape):
    nd = len(shape)
    return pl.BlockSpec(shape, lambda *_: (0,) * nd, pipeline_mode=pl.Buffered(1))


def _params(n_axes):
    return pltpu.CompilerParams(dimension_semantics=("arbitrary",) * n_axes,
                                vmem_limit_bytes=VMEM_LIMIT_BYTES)


def _layer(x, pos3, inv_freq, mix_norm, w_in, a_v_norm_g, a_v_norm_b, a_spatial_w, a_spatial_b,
           q_a_norm, w_uq, kv_a_norm, w_ukv, w_out, ffn_norm, w_up, conv_w, conv_b, w_down,
           final_norm, apply_final):
    batch, seq, d_model = x.shape
    heads = MLA_HEADS
    a_w = A_GROUPS * A_GROUP_DIM
    d_ff = w_down.shape[0]
    assert d_model == a_w == heads * V_HEAD_DIM
    assert seq % TOKEN_TILE == 0 and TOKEN_TILE % CHUNK == 0 and seq % Q_TILE == 0
    assert d_ff % FF_CHUNK == 0
    tm = TOKEN_TILE
    n_tiles = seq // tm

    s_ckv = 2 * a_w + Q_LORA_RANK
    s_kr = s_ckv + KV_LORA_RANK
    s_ga = s_kr + QK_ROPE_DIM
    win_p = jnp.concatenate([w_in[:, :s_ga], w_in[:, s_kr:s_ga], w_in[:, s_ga:]], axis=1).astype(BF16)
    wuq3 = w_uq.reshape(Q_LORA_RANK, heads, QK_HEAD_DIM)
    wuq_p = jnp.concatenate([wuq3[:, :, :QK_NOPE_DIM].reshape(Q_LORA_RANK, heads * QK_NOPE_DIM),
                             wuq3[:, :, QK_NOPE_DIM:].reshape(Q_LORA_RANK, heads * QK_ROPE_DIM)],
                            axis=1).astype(BF16)
    wukv3 = w_ukv.reshape(KV_LORA_RANK, heads, QK_NOPE_DIM + V_HEAD_DIM)
    wukv_p = jnp.concatenate([wukv3[:, :, :QK_NOPE_DIM].reshape(KV_LORA_RANK, heads * QK_NOPE_DIM),
                              wukv3[:, :, QK_NOPE_DIM:].reshape(KV_LORA_RANK, heads * V_HEAD_DIM)],
                             axis=1).astype(BF16)
    asb_full = jnp.repeat(a_spatial_b.T, A_GROUP_DIM, axis=1)
    row = lambda a: a.reshape(1, -1)

    tok = lambda w: pl.BlockSpec((1, tm, w), lambda b, i: (b, i, 0))
    tok_out = lambda w: jax.ShapeDtypeStruct((batch, seq, w), BF16)
    n_rope_blk = (heads // 2) * LANES

    ya, gb, qn, qr, kn, kr, v = pl.pallas_call(
        functools.partial(_mixer_in_kernel, d_model=d_model, scale=QK_HEAD_DIM ** -0.5),
        grid=(batch, n_tiles),
        in_specs=[tok(d_model),
                  pl.BlockSpec((1, tm, 1), lambda b, i: (b, i, 0)),
                  _const_spec((1, LANES)), _const_spec((1, d_model)), _const_spec(win_p.shape),
                  _const_spec((1, a_w)), _const_spec((1, a_w)), _const_spec(a_spatial_w.shape),
                  _const_spec(asb_full.shape), _const_spec((1, Q_LORA_RANK)), _const_spec(wuq_p.shape),
                  _const_spec((1, KV_LORA_RANK)), _const_spec(wukv_p.shape)],
        out_specs=[tok(a_w), tok(d_model), tok(heads * QK_NOPE_DIM), tok(n_rope_blk),
                   tok(heads * QK_NOPE_DIM),
                   pl.BlockSpec((1, 2, tm, LANES), lambda b, i: (b, 0, i, 0)),
                   tok(heads * V_HEAD_DIM)],
        out_shape=[tok_out(a_w), tok_out(d_model), tok_out(heads * QK_NOPE_DIM), tok_out(n_rope_blk),
                   tok_out(heads * QK_NOPE_DIM),
                   jax.ShapeDtypeStruct((batch, 2, seq, LANES), BF16),
                   tok_out(heads * V_HEAD_DIM)],
        compiler_params=_params(2),
        name="mixer_in",
    )(x, pos3, inv_freq, row(mix_norm), win_p, row(a_v_norm_g), row(a_v_norm_b), a_spatial_w, asb_full,
      row(q_a_norm), wuq_p, row(kv_a_norm), wukv_p)

    head_blk = pl.BlockSpec((1, seq, LANES), lambda b, h: (b, 0, h))
    merged = pl.pallas_call(
        functools.partial(_attention_kernel, tq=Q_TILE),
        grid=(batch, heads),
        in_specs=[head_blk,
                  pl.BlockSpec((1, seq, LANES), lambda b, h: (b, 0, h // 2)),
                  head_blk,
                  pl.BlockSpec((1, 1, seq, LANES), lambda b, h: (b, h % 2, 0, 0)),
                  head_blk, head_blk, head_blk],
        out_specs=head_blk,
        out_shape=jax.ShapeDtypeStruct((batch, seq, d_model), BF16),
        scratch_shapes=[pltpu.VMEM((seq, 2 * LANES), BF16), pltpu.VMEM((V_HEAD_DIM, seq), BF16)],
        compiler_params=_params(2),
        name="attention",
    )(qn, qr, kn, kr, v, ya, gb)

    out = pl.pallas_call(
        functools.partial(_ffn_kernel, d_ff=d_ff, apply_final=apply_final),
        grid=(batch, n_tiles),
        in_specs=[tok(d_model), tok(d_model),
                  _const_spec((d_model, d_model)), _const_spec((1, d_model)),
                  _const_spec((d_model, 2 * d_ff)), _const_spec((CONV_WIDTH, 2 * d_ff)),
                  _const_spec((1, 2 * d_ff)), _const_spec((d_ff, d_model)), _const_spec((1, d_model))],
        out_specs=tok(d_model),
        out_shape=jax.ShapeDtypeStruct((batch, seq, d_model), F32),
        scratch_shapes=[pltpu.VMEM((SUBLANES + tm, 2 * d_ff), F32), pltpu.VMEM((tm, d_ff), BF16)],
        compiler_params=_params(2),
        name="ffn",
    )(x, merged, w_out.astype(BF16), row(ffn_norm), w_up.astype(BF16), conv_w, row(conv_b),
      w_down.astype(BF16), row(final_norm))
    return out


def kernel(x, positions, mix_norm, w_in, a_v_norm_g, a_v_norm_b, a_spatial_w, a_spatial_b, q_a_norm, w_uq,
           kv_a_norm, w_ukv, w_out, ffn_norm, w_up, conv_w, conv_b, w_down, final_norm):
    depth = w_in.shape[0]
    batch, seq, _ = x.shape
    pos3 = positions.reshape(batch, seq, 1)
    inv_freq = 1.0 / (ROPE_THETA ** (jnp.arange(0, QK_ROPE_DIM, 2, dtype=F32) / QK_ROPE_DIM))
    inv_freq = jnp.tile(inv_freq, LANES // (QK_ROPE_DIM // 2)).reshape(1, LANES)
    for l in range(depth):
        x = _layer(x, pos3, inv_freq, mix_norm[l], w_in[l], a_v_norm_g[l], a_v_norm_b[l], a_spatial_w[l],
                   a_spatial_b[l], q_a_norm[l], w_uq[l], kv_a_norm[l], w_ukv[l], w_out[l], ffn_norm[l],
                   w_up[l], conv_w[l], conv_b[l], w_down[l], final_norm, apply_final=(l == depth - 1))
    return x
```

```python
import functools

import jax
import jax.numpy as jnp
from jax import lax
from jax.experimental import pallas as pl
from jax.experimental.pallas import tpu as pltpu

EPS = 1e-6
A_GROUPS = 8
A_GROUP_DIM = 128
CHUNK = 128
MLA_HEADS = 8
QK_NOPE_DIM = 128
QK_ROPE_DIM = 64
QK_HEAD_DIM = QK_NOPE_DIM + QK_ROPE_DIM
V_HEAD_DIM = 128
Q_LORA_RANK = 256
KV_LORA_RANK = 128
ROPE_THETA = 10000.0
CONV_WIDTH = 3

LANES = 128
SUBLANES = 8
VMEM_LIMIT_BYTES = 56 * 1024 * 1024

TOKEN_TILE = 512
ATTN_HEADS_PER_STEP = 2
ATTN_LOOKAHEAD = 3
LOG2_E = 1.4426950408889634
FF_CHUNK = 256

F32 = jnp.float32
BF16 = jnp.bfloat16


def _rms(x, g):
    return x * lax.rsqrt(jnp.mean(x * x, axis=-1, keepdims=True) + EPS) * g


def _dot(a, b):
    return jnp.dot(a, b, preferred_element_type=F32)


def _mixer_in_kernel(x_ref, pos_ref, posr_ref, invf_ref, invfc_ref, mixn_ref, win_ref, avg_ref, avb_ref,
                     asw_ref, asb_ref, qan_ref, wuqt_ref, kvn_ref, wuk_ref, wuvt_ref,
                     ya_ref, gb_ref, qt_ref, k_ref, vt_ref, *, d_model, scale):
    tm = x_ref.shape[1]
    a_w = A_GROUPS * A_GROUP_DIM
    o_u, o_v = 0, a_w
    o_cq = 2 * a_w
    o_ckv = o_cq + Q_LORA_RANK
    o_kr = o_ckv + KV_LORA_RANK
    o_ga = o_kr + 2 * QK_ROPE_DIM
    o_gb = o_ga + d_model
    o_end = o_gb + d_model

    x = x_ref[0]
    h = _rms(x, mixn_ref[...]).astype(BF16)

    def proj(lo, hi):
        return _dot(h, win_ref[:, lo:hi])

    z_cq, z_ckv, z_kr = proj(o_cq, o_ckv), proj(o_ckv, o_kr), proj(o_kr, o_ga)
    z_v, z_u, z_ga, z_gb = proj(o_v, o_cq), proj(o_u, o_v), proj(o_ga, o_gb), proj(o_gb, o_end)

    nt = (((1,), (1,)), ((), ()))
    cqn = _rms(z_cq, qan_ref[...]).astype(BF16)
    ckvn = _rms(z_ckv, kvn_ref[...]).astype(BF16)
    q_t = lax.dot_general(wuqt_ref[...], cqn, nt, preferred_element_type=F32)
    k_nope = _dot(ckvn, wuk_ref[...])
    vt_ref[0, 0] = lax.dot_general(wuvt_ref[...], ckvn, nt, preferred_element_type=F32).astype(BF16)

    v_a = jax.nn.gelu(z_v)
    mu = jnp.mean(v_a, axis=-1, keepdims=True)
    var = jnp.mean(jnp.square(v_a - mu), axis=-1, keepdims=True)
    v_ln = (((v_a - mu) * lax.rsqrt(var + EPS)) * avg_ref[...] + avb_ref[...]).astype(BF16)
    pre = jax.nn.gelu(z_u) * jax.nn.sigmoid(z_ga)
    gb_ref[0] = jax.nn.sigmoid(z_gb).astype(BF16)

    t_idx = lax.broadcasted_iota(jnp.int32, (CHUNK, CHUNK), 0)
    s_idx = lax.broadcasted_iota(jnp.int32, (CHUNK, CHUNK), 1)
    causal = s_idx <= t_idx
    for g in range(A_GROUPS):
        w_g = jnp.where(causal, asw_ref[g], 0.0).astype(BF16)
        cols = slice(g * A_GROUP_DIM, (g + 1) * A_GROUP_DIM)
        for c in range(tm // CHUNK):
            rows = slice(c * CHUNK, (c + 1) * CHUNK)
            mixed = _dot(w_g, v_ln[rows, cols]) + asb_ref[:, cols]
            ya_ref[0, rows, cols] = (pre[rows, cols] * mixed).astype(BF16)

    ang = pos_ref[0].astype(F32) * invf_ref[...]
    cos = jnp.cos(ang)
    sin = jnp.sin(ang)
    lane = lax.broadcasted_iota(jnp.int32, (1, LANES), 1)
    first_half = (lane % QK_ROPE_DIM) < (QK_ROPE_DIM // 2)
    sin_x1 = jnp.where(first_half, -sin, 0.0)
    sin_x2 = jnp.where(first_half, 0.0, sin)
    half = QK_ROPE_DIM // 2

    def rope(blk):
        return (blk * cos + pltpu.roll(blk, LANES - half, 1) * sin_x1
                + pltpu.roll(blk, half, 1) * sin_x2)

    n_nope = MLA_HEADS * QK_NOPE_DIM
    ang_t = invfc_ref[...] * posr_ref[0].astype(F32)
    cos_t = jnp.cos(ang_t)
    sin_t = jnp.sin(ang_t)
    for p in range(MLA_HEADS // 2):
        parts = []
        for o in (0, QK_ROPE_DIM):
            r = n_nope + p * LANES + o
            x1, x2 = q_t[r:r + half], q_t[r + half:r + 2 * half]
            parts += [x1 * cos_t - x2 * sin_t, x1 * sin_t + x2 * cos_t]
        q_rope = (jnp.concatenate(parts, axis=0) * scale).astype(BF16)
        for hd in (2 * p, 2 * p + 1):
            qt_ref[0, hd, 0:LANES, :] = (q_t[hd * QK_NOPE_DIM:(hd + 1) * QK_NOPE_DIM] * scale).astype(BF16)
            qt_ref[0, hd, LANES:2 * LANES, :] = q_rope

    k_rot = rope(z_kr)
    k_rope = [jnp.where(lane < QK_ROPE_DIM, k_rot, 0.0).astype(BF16),
              jnp.where(lane >= QK_ROPE_DIM, k_rot, 0.0).astype(BF16)]
    for hd in range(MLA_HEADS):
        k_ref[0, hd, :, 0:LANES] = k_nope[:, hd * QK_NOPE_DIM:(hd + 1) * QK_NOPE_DIM].astype(BF16)
        k_ref[0, hd, :, LANES:2 * LANES] = k_rope[hd % 2]


def _attention_kernel(qt_ref, k_ref, vt_ref, ya_ref, gb_ref, o_ref, *, blk):
    hb, seq = k_ref.shape[1], k_ref.shape[2]
    key_i = lax.broadcasted_iota(jnp.int32, (blk, blk), 0)
    qry_i = lax.broadcasted_iota(jnp.int32, (blk, blk), 1)
    visible = key_i <= qry_i

    def scores(qi, kj, h):
        return _dot(k_ref[0, h, kj * blk:(kj + 1) * blk, :], qt_ref[0, h, :, qi * blk:(qi + 1) * blk])

    order = [(qi, kj, h) for qi in range(seq // blk) for kj in range(qi + 1) for h in range(hb)]
    pending = {t: scores(*order[t]) for t in range(min(ATTN_LOOKAHEAD, len(order)))}
    state = {}
    for t, (qi, kj, h) in enumerate(order):
        if t + ATTN_LOOKAHEAD < len(order):
            pending[t + ATTN_LOOKAHEAD] = scores(*order[t + ATTN_LOOKAHEAD])
        s = pending.pop(t)
        rows = slice(qi * blk, (qi + 1) * blk)
        vrows = slice(h * V_HEAD_DIM, (h + 1) * V_HEAD_DIM)
        if kj == qi:
            s = jnp.where(visible, s, -jnp.inf)
        m_blk = jnp.max(s, axis=0, keepdims=True)
        if kj == 0:
            m_new = m_blk
        else:
            m, l, acc = state[h]
            m_new = jnp.maximum(m, m_blk)
        p = jnp.exp2(s - m_new)
        l_blk = jnp.sum(p, axis=0, keepdims=True)
        pv = _dot(vt_ref[0, kj, vrows, :], p.astype(BF16))
        if kj == 0:
            l, acc = l_blk, pv
        else:
            alpha = jnp.exp2(m - m_new)
            l, acc = alpha * l + l_blk, alpha * acc + pv
        state[h] = (m_new, l, acc)
        if kj == qi:
            y_b = (acc / l).T
            merged = ya_ref[0, rows, vrows].astype(F32) + gb_ref[0, rows, vrows].astype(F32) * y_b
            o_ref[0, rows, vrows] = merged.astype(BF16)


def _ffn_kernel(x_ref, mg_ref, wout_ref, ffn_ref, wup_ref, cw_ref, cb_ref, wdown_ref, fin_ref,
                o_ref, up_scr, act_scr, *, d_ff, apply_final):
    tm = x_ref.shape[1]
    halo = SUBLANES
    j = pl.program_id(1)

    @pl.when(j == 0)
    def _():
        up_scr[0:halo, :] = jnp.zeros((halo, up_scr.shape[1]), F32)

    @pl.when(j > 0)
    def _():
        up_scr[0:halo, :] = up_scr[tm:tm + halo, :]

    x1 = x_ref[0] + _dot(mg_ref[0], wout_ref[...])
    o_ref[0] = x1
    up_scr[halo:halo + tm, :] = _dot(_rms(x1, ffn_ref[...]).astype(BF16), wup_ref[...])

    def conv(lo):
        cols = slice(lo, lo + FF_CHUNK)
        taps = 0.0
        for k in range(CONV_WIDTH):
            r0 = halo - (CONV_WIDTH - 1) + k
            taps = taps + cw_ref[k:k + 1, cols] * up_scr[r0:r0 + tm, cols]
        return cb_ref[:, cols] + taps

    for c in range(0, d_ff, FF_CHUNK):
        act_scr[:, c:c + FF_CHUNK] = (jax.nn.silu(conv(c)) * conv(d_ff + c)).astype(BF16)

    x2 = o_ref[0] + _dot(act_scr[...], wdown_ref[...])
    o_ref[0] = _rms(x2, fin_ref[...]) if apply_final else x2


def _const_spec(shape):
    nd = len(shape)
    return pl.BlockSpec(shape, lambda *_: (0,) * nd, pipeline_mode=pl.Buffered(1))


def _params(n_axes):
    return pltpu.CompilerParams(dimension_semantics=("arbitrary",) * n_axes,
                                vmem_limit_bytes=VMEM_LIMIT_BYTES)


def _layer(x, pos3, pos_row, inv_freq, inv_freq_col, mix_norm, w_in, a_v_norm_g, a_v_norm_b, a_spatial_w, a_spatial_b,
           q_a_norm, w_uq, kv_a_norm, w_ukv, w_out, ffn_norm, w_up, conv_w, conv_b, w_down,
           final_norm, apply_final):
    batch, seq, d_model = x.shape
    heads = MLA_HEADS
    a_w = A_GROUPS * A_GROUP_DIM
    d_ff = w_down.shape[0]
    assert d_model == a_w == heads * V_HEAD_DIM
    assert seq % TOKEN_TILE == 0 and TOKEN_TILE % CHUNK == 0
    assert d_ff % FF_CHUNK == 0
    tm = TOKEN_TILE
    n_tiles = seq // tm

    s_ckv = 2 * a_w + Q_LORA_RANK
    s_kr = s_ckv + KV_LORA_RANK
    s_ga = s_kr + QK_ROPE_DIM
    win_p = jnp.concatenate([w_in[:, :s_ga], w_in[:, s_kr:s_ga], w_in[:, s_ga:]], axis=1).astype(BF16)
    wuq3 = w_uq.reshape(Q_LORA_RANK, heads, QK_HEAD_DIM)
    wuq_p = jnp.concatenate([wuq3[:, :, :QK_NOPE_DIM].reshape(Q_LORA_RANK, heads * QK_NOPE_DIM),
                             wuq3[:, :, QK_NOPE_DIM:].reshape(Q_LORA_RANK, heads * QK_ROPE_DIM)],
                            axis=1).astype(BF16)
    wukv3 = w_ukv.reshape(KV_LORA_RANK, heads, QK_NOPE_DIM + V_HEAD_DIM)
    wuk_p = wukv3[:, :, :QK_NOPE_DIM].reshape(KV_LORA_RANK, heads * QK_NOPE_DIM).astype(BF16)
    wuvt_p = wukv3[:, :, QK_NOPE_DIM:].reshape(KV_LORA_RANK, heads * V_HEAD_DIM).T.astype(BF16)
    asb_full = jnp.repeat(a_spatial_b.T, A_GROUP_DIM, axis=1)
    row = lambda a: a.reshape(1, -1)

    tok = lambda w: pl.BlockSpec((1, tm, w), lambda b, i: (b, i, 0))
    tok_out = lambda w: jax.ShapeDtypeStruct((batch, seq, w), BF16)
    qk_w = 2 * LANES
    wuqt_p = wuq_p.T

    q_scale = QK_HEAD_DIM ** -0.5 * LOG2_E
    ya, gb, qt, k, vt = pl.pallas_call(
        functools.partial(_mixer_in_kernel, d_model=d_model, scale=q_scale),
        grid=(batch, n_tiles),
        in_specs=[tok(d_model),
                  pl.BlockSpec((1, tm, 1), lambda b, i: (b, i, 0)),
                  pl.BlockSpec((1, 1, tm), lambda b, i: (b, 0, i)),
                  _const_spec((1, LANES)), _const_spec((QK_ROPE_DIM // 2, 1)),
                  _const_spec((1, d_model)), _const_spec(win_p.shape),
                  _const_spec((1, a_w)), _const_spec((1, a_w)), _const_spec(a_spatial_w.shape),
                  _const_spec(asb_full.shape), _const_spec((1, Q_LORA_RANK)), _const_spec(wuqt_p.shape),
                  _const_spec((1, KV_LORA_RANK)), _const_spec(wuk_p.shape), _const_spec(wuvt_p.shape)],
        out_specs=[tok(a_w), tok(d_model),
                   pl.BlockSpec((1, heads, qk_w, tm), lambda b, i: (b, 0, 0, i)),
                   pl.BlockSpec((1, heads, tm, qk_w), lambda b, i: (b, 0, i, 0)),
                   pl.BlockSpec((1, 1, heads * V_HEAD_DIM, tm), lambda b, i: (b, i, 0, 0))],
        out_shape=[tok_out(a_w), tok_out(d_model),
                   jax.ShapeDtypeStruct((batch, heads, qk_w, seq), BF16),
                   jax.ShapeDtypeStruct((batch, heads, seq, qk_w), BF16),
                   jax.ShapeDtypeStruct((batch, n_tiles, heads * V_HEAD_DIM, tm), BF16)],
        compiler_params=_params(2),
        name="mixer_in",
    )(x, pos3, pos_row, inv_freq, inv_freq_col, row(mix_norm), win_p, row(a_v_norm_g), row(a_v_norm_b),
      a_spatial_w, asb_full, row(q_a_norm), wuqt_p, row(kv_a_norm), wuk_p, wuvt_p)

    hb = ATTN_HEADS_PER_STEP
    assert heads % hb == 0
    grp_tok = pl.BlockSpec((1, seq, hb * V_HEAD_DIM), lambda b, g: (b, 0, g))
    merged = pl.pallas_call(
        functools.partial(_attention_kernel, blk=tm),
        grid=(batch, heads // hb),
        in_specs=[pl.BlockSpec((1, hb, qk_w, seq), lambda b, g: (b, g, 0, 0)),
                  pl.BlockSpec((1, hb, seq, qk_w), lambda b, g: (b, g, 0, 0)),
                  pl.BlockSpec((1, n_tiles, hb * V_HEAD_DIM, tm), lambda b, g: (b, 0, g, 0)),
                  grp_tok, grp_tok],
        out_specs=grp_tok,
        out_shape=jax.ShapeDtypeStruct((batch, seq, d_model), BF16),
        compiler_params=_params(2),
        name="attention",
    )(qt, k, vt, ya, gb)

    out = pl.pallas_call(
        functools.partial(_ffn_kernel, d_ff=d_ff, apply_final=apply_final),
        grid=(batch, n_tiles),
        in_specs=[tok(d_model), tok(d_model),
                  _const_spec((d_model, d_model)), _const_spec((1, d_model)),
                  _const_spec((d_model, 2 * d_ff)), _const_spec((CONV_WIDTH, 2 * d_ff)),
                  _const_spec((1, 2 * d_ff)), _const_spec((d_ff, d_model)), _const_spec((1, d_model))],
        out_specs=tok(d_model),
        out_shape=jax.ShapeDtypeStruct((batch, seq, d_model), F32),
        scratch_shapes=[pltpu.VMEM((SUBLANES + tm, 2 * d_ff), F32), pltpu.VMEM((tm, d_ff), BF16)],
        compiler_params=_params(2),
        name="ffn",
    )(x, merged, w_out.astype(BF16), row(ffn_norm), w_up.astype(BF16), conv_w, row(conv_b),
      w_down.astype(BF16), row(final_norm))
    return out


def kernel(x, positions, mix_norm, w_in, a_v_norm_g, a_v_norm_b, a_spatial_w, a_spatial_b, q_a_norm, w_uq,
           kv_a_norm, w_ukv, w_out, ffn_norm, w_up, conv_w, conv_b, w_down, final_norm):
    depth = w_in.shape[0]
    batch, seq, _ = x.shape
    pos3 = positions.reshape(batch, seq, 1)
    pos_row = positions.reshape(batch, 1, seq)
    inv_freq_col = (1.0 / (ROPE_THETA ** (jnp.arange(0, QK_ROPE_DIM, 2, dtype=F32) / QK_ROPE_DIM))).reshape(-1, 1)
    inv_freq = jnp.tile(inv_freq_col.reshape(-1), LANES // (QK_ROPE_DIM // 2)).reshape(1, LANES)
    for l in range(depth):
        x = _layer(x, pos3, pos_row, inv_freq, inv_freq_col, mix_norm[l], w_in[l], a_v_norm_g[l], a_v_norm_b[l], a_spatial_w[l],
                   a_spatial_b[l], q_a_norm[l], w_uq[l], kv_a_norm[l], w_ukv[l], w_out[l], ffn_norm[l],
                   w_up[l], conv_w[l], conv_b[l], w_down[l], final_norm, apply_final=(l == depth - 1))
    return x
```

```python
import functools

import jax
import jax.numpy as jnp
from jax import lax
from jax.experimental import pallas as pl
from jax.experimental.pallas import tpu as pltpu

EPS = 1e-6
A_GROUPS = 8
A_GROUP_DIM = 128
CHUNK = 128
MLA_HEADS = 8
QK_NOPE_DIM = 128
QK_ROPE_DIM = 64
QK_HEAD_DIM = QK_NOPE_DIM + QK_ROPE_DIM
V_HEAD_DIM = 128
Q_LORA_RANK = 256
KV_LORA_RANK = 128
ROPE_THETA = 10000.0
CONV_WIDTH = 3

LANES = 128
SUBLANES = 8
VMEM_LIMIT_BYTES = 56 * 1024 * 1024

TOKEN_TILE = 512
ATTN_HEADS_PER_STEP = 2
ATTN_LOOKAHEAD = 3
LOG2_E = 1.4426950408889634
FF_CHUNK = 256

F32 = jnp.float32
BF16 = jnp.bfloat16


def _rms(x, g):
    return x * lax.rsqrt(jnp.mean(x * x, axis=-1, keepdims=True) + EPS) * g


def _dot(a, b):
    return jnp.dot(a, b, preferred_element_type=F32)


def _mixer_in_kernel(x_ref, pos_ref, posr_ref, invf_ref, invfc_ref, mixn_ref, win_ref, avg_ref, avb_ref,
                     asw_ref, asb_ref, qan_ref, wuqt_ref, kvn_ref, wuk_ref, wuvt_ref,
                     ya_ref, gb_ref, qt_ref, k_ref, vt_ref, *, d_model, scale):
    tm = x_ref.shape[1]
    a_w = A_GROUPS * A_GROUP_DIM
    o_u, o_v = 0, a_w
    o_cq = 2 * a_w
    o_ckv = o_cq + Q_LORA_RANK
    o_kr = o_ckv + KV_LORA_RANK
    o_ga = o_kr + 2 * QK_ROPE_DIM
    o_gb = o_ga + d_model
    o_end = o_gb + d_model

    x = x_ref[0]
    h = _rms(x, mixn_ref[...]).astype(BF16)

    def proj(lo, hi):
        return _dot(h, win_ref[:, lo:hi])

    z_cq, z_ckv, z_kr = proj(o_cq, o_ckv), proj(o_ckv, o_kr), proj(o_kr, o_ga)
    z_v, z_u, z_ga, z_gb = proj(o_v, o_cq), proj(o_u, o_v), proj(o_ga, o_gb), proj(o_gb, o_end)

    nt = (((1,), (1,)), ((), ()))
    cqn = _rms(z_cq, qan_ref[...]).astype(BF16)
    ckvn = _rms(z_ckv, kvn_ref[...]).astype(BF16)
    q_t = lax.dot_general(wuqt_ref[...], cqn, nt, preferred_element_type=F32)
    k_nope = _dot(ckvn, wuk_ref[...])
    vt_ref[0, 0] = lax.dot_general(wuvt_ref[...], ckvn, nt, preferred_element_type=F32).astype(BF16)

    v_a = jax.nn.gelu(z_v)
    mu = jnp.mean(v_a, axis=-1, keepdims=True)
    var = jnp.mean(jnp.square(v_a - mu), axis=-1, keepdims=True)
    v_ln = (((v_a - mu) * lax.rsqrt(var + EPS)) * avg_ref[...] + avb_ref[...]).astype(BF16)
    pre = jax.nn.gelu(z_u) * jax.nn.sigmoid(z_ga)
    gate_b = jax.nn.sigmoid(z_gb).astype(BF16)
    gw = gb_ref.shape[3]
    for grp in range(gb_ref.shape[1]):
        gb_ref[0, grp] = gate_b[:, grp * gw:(grp + 1) * gw]

    t_idx = lax.broadcasted_iota(jnp.int32, (CHUNK, CHUNK), 0)
    s_idx = lax.broadcasted_iota(jnp.int32, (CHUNK, CHUNK), 1)
    causal = s_idx <= t_idx
    for g in range(A_GROUPS):
        w_g = jnp.where(causal, asw_ref[g], 0.0).astype(BF16)
        cols = slice(g * A_GROUP_DIM, (g + 1) * A_GROUP_DIM)
        for c in range(tm // CHUNK):
            rows = slice(c * CHUNK, (c + 1) * CHUNK)
            mixed = _dot(w_g, v_ln[rows, cols]) + asb_ref[:, cols]
            lo = (g * A_GROUP_DIM) % gw
            ya_ref[0, (g * A_GROUP_DIM) // gw, rows, lo:lo + A_GROUP_DIM] = (pre[rows, cols] * mixed).astype(BF16)

    ang = pos_ref[0].astype(F32) * invf_ref[...]
    cos = jnp.cos(ang)
    sin = jnp.sin(ang)
    lane = lax.broadcasted_iota(jnp.int32, (1, LANES), 1)
    first_half = (lane % QK_ROPE_DIM) < (QK_ROPE_DIM // 2)
    sin_x1 = jnp.where(first_half, -sin, 0.0)
    sin_x2 = jnp.where(first_half, 0.0, sin)
    half = QK_ROPE_DIM // 2

    def rope(blk):
        return (blk * cos + pltpu.roll(blk, LANES - half, 1) * sin_x1
                + pltpu.roll(blk, half, 1) * sin_x2)

    n_nope = MLA_HEADS * QK_NOPE_DIM
    ang_t = invfc_ref[...] * posr_ref[0].astype(F32)
    cos_t = jnp.cos(ang_t)
    sin_t = jnp.sin(ang_t)
    for p in range(MLA_HEADS // 2):
        parts = []
        for o in (0, QK_ROPE_DIM):
            r = n_nope + p * LANES + o
            x1, x2 = q_t[r:r + half], q_t[r + half:r + 2 * half]
            parts += [x1 * cos_t - x2 * sin_t, x1 * sin_t + x2 * cos_t]
        q_rope = (jnp.concatenate(parts, axis=0) * scale).astype(BF16)
        for hd in (2 * p, 2 * p + 1):
            qt_ref[0, 0, hd, 0:LANES, :] = (q_t[hd * QK_NOPE_DIM:(hd + 1) * QK_NOPE_DIM] * scale).astype(BF16)
            qt_ref[0, 0, hd, LANES:2 * LANES, :] = q_rope

    k_rot = rope(z_kr)
    k_rope = [jnp.where(lane < QK_ROPE_DIM, k_rot, 0.0).astype(BF16),
              jnp.where(lane >= QK_ROPE_DIM, k_rot, 0.0).astype(BF16)]
    for hd in range(MLA_HEADS):
        k_ref[0, hd, :, 0:LANES] = k_nope[:, hd * QK_NOPE_DIM:(hd + 1) * QK_NOPE_DIM].astype(BF16)
        k_ref[0, hd, :, LANES:2 * LANES] = k_rope[hd % 2]


def _attention_kernel(qt_ref, k_ref, vt_ref, ya_ref, gb_ref, o_ref, *, blk):
    hb, seq = k_ref.shape[1], k_ref.shape[2]
    key_i = lax.broadcasted_iota(jnp.int32, (blk, blk), 0)
    qry_i = lax.broadcasted_iota(jnp.int32, (blk, blk), 1)
    visible = key_i <= qry_i

    def scores(qi, kj, h):
        return _dot(k_ref[0, h, kj * blk:(kj + 1) * blk, :], qt_ref[0, qi, h])

    order = [(qi, kj, h) for qi in range(seq // blk) for kj in range(qi + 1) for h in range(hb)]
    pending = {t: scores(*order[t]) for t in range(min(ATTN_LOOKAHEAD, len(order)))}
    state = {}
    for t, (qi, kj, h) in enumerate(order):
        if t + ATTN_LOOKAHEAD < len(order):
            pending[t + ATTN_LOOKAHEAD] = scores(*order[t + ATTN_LOOKAHEAD])
        s = pending.pop(t)
        rows = slice(qi * blk, (qi + 1) * blk)
        vrows = slice(h * V_HEAD_DIM, (h + 1) * V_HEAD_DIM)
        if kj == qi:
            s = jnp.where(visible, s, -jnp.inf)
        m_blk = jnp.max(s, axis=0, keepdims=True)
        if kj == 0:
            m_new = m_blk
        else:
            m, l, acc = state[h]
            m_new = jnp.maximum(m, m_blk)
        p = jnp.exp2(s - m_new)
        l_blk = jnp.sum(p, axis=0, keepdims=True)
        pv = _dot(vt_ref[0, kj, vrows, :], p.astype(BF16))
        if kj == 0:
            l, acc = l_blk, pv
        else:
            alpha = jnp.exp2(m - m_new)
            l, acc = alpha * l + l_blk, alpha * acc + pv
        state[h] = (m_new, l, acc)
        if kj == qi:
            y_b = (acc / l).T
            merged = ya_ref[0, 0, rows, vrows].astype(F32) + gb_ref[0, 0, rows, vrows].astype(F32) * y_b
            o_ref[0, 0, rows, vrows] = merged.astype(BF16)


def _ffn_kernel(x_ref, mg_ref, wout_ref, ffn_ref, wup_ref, cw_ref, cb_ref, wdown_ref, fin_ref,
                o_ref, up_scr, act_scr, *, d_ff, apply_final):
    tm = x_ref.shape[1]
    halo = SUBLANES
    j = pl.program_id(1)

    @pl.when(j == 0)
    def _():
        up_scr[0:halo, :] = jnp.zeros((halo, up_scr.shape[1]), F32)

    @pl.when(j > 0)
    def _():
        up_scr[0:halo, :] = up_scr[tm:tm + halo, :]

    x1 = x_ref[0]
    gw = mg_ref.shape[3]
    for grp in range(mg_ref.shape[1]):
        x1 = x1 + _dot(mg_ref[0, grp], wout_ref[grp * gw:(grp + 1) * gw, :])
    o_ref[0] = x1
    up_scr[halo:halo + tm, :] = _dot(_rms(x1, ffn_ref[...]).astype(BF16), wup_ref[...])

    def conv(lo):
        cols = slice(lo, lo + FF_CHUNK)
        taps = 0.0
        for k in range(CONV_WIDTH):
            r0 = halo - (CONV_WIDTH - 1) + k
            taps = taps + cw_ref[k:k + 1, cols] * up_scr[r0:r0 + tm, cols]
        return cb_ref[:, cols] + taps

    for c in range(0, d_ff, FF_CHUNK):
        act_scr[:, c:c + FF_CHUNK] = (jax.nn.silu(conv(c)) * conv(d_ff + c)).astype(BF16)

    x2 = o_ref[0] + _dot(act_scr[...], wdown_ref[...])
    o_ref[0] = _rms(x2, fin_ref[...]) if apply_final else x2


def _const_spec(shape):
    nd = len(shape)
    return pl.BlockSpec(shape, lambda *_: (0,) * nd, pipeline_mode=pl.Buffered(1))


def _params(n_axes):
    return pltpu.CompilerParams(dimension_semantics=("arbitrary",) * n_axes,
                                vmem_limit_bytes=VMEM_LIMIT_BYTES)


def _layer(x, pos3, pos_row, inv_freq, inv_freq_col, mix_norm, w_in, a_v_norm_g, a_v_norm_b, a_spatial_w, a_spatial_b,
           q_a_norm, w_uq, kv_a_norm, w_ukv, w_out, ffn_norm, w_up, conv_w, conv_b, w_down,
           final_norm, apply_final):
    batch, seq, d_model = x.shape
    heads = MLA_HEADS
    a_w = A_GROUPS * A_GROUP_DIM
    d_ff = w_down.shape[0]
    assert d_model == a_w == heads * V_HEAD_DIM
    assert seq % TOKEN_TILE == 0 and TOKEN_TILE % CHUNK == 0
    assert d_ff % FF_CHUNK == 0
    tm = TOKEN_TILE
    n_tiles = seq // tm

    s_ckv = 2 * a_w + Q_LORA_RANK
    s_kr = s_ckv + KV_LORA_RANK
    s_ga = s_kr + QK_ROPE_DIM
    win_p = jnp.concatenate([w_in[:, :s_ga], w_in[:, s_kr:s_ga], w_in[:, s_ga:]], axis=1).astype(BF16)
    wuq3 = w_uq.reshape(Q_LORA_RANK, heads, QK_HEAD_DIM)
    wuq_p = jnp.concatenate([wuq3[:, :, :QK_NOPE_DIM].reshape(Q_LORA_RANK, heads * QK_NOPE_DIM),
                             wuq3[:, :, QK_NOPE_DIM:].reshape(Q_LORA_RANK, heads * QK_ROPE_DIM)],
                            axis=1).astype(BF16)
    wukv3 = w_ukv.reshape(KV_LORA_RANK, heads, QK_NOPE_DIM + V_HEAD_DIM)
    wuk_p = wukv3[:, :, :QK_NOPE_DIM].reshape(KV_LORA_RANK, heads * QK_NOPE_DIM).astype(BF16)
    wuvt_p = wukv3[:, :, QK_NOPE_DIM:].reshape(KV_LORA_RANK, heads * V_HEAD_DIM).T.astype(BF16)
    asb_full = jnp.repeat(a_spatial_b.T, A_GROUP_DIM, axis=1)
    row = lambda a: a.reshape(1, -1)

    tok = lambda w: pl.BlockSpec((1, tm, w), lambda b, i: (b, i, 0))
    qk_w = 2 * LANES
    wuqt_p = wuq_p.T
    hb = ATTN_HEADS_PER_STEP
    assert heads % hb == 0
    n_grp, gw = heads // hb, hb * V_HEAD_DIM
    grp_arr = jax.ShapeDtypeStruct((batch, n_grp, seq, gw), BF16)
    grp_tile = pl.BlockSpec((1, n_grp, tm, gw), lambda b, i: (b, 0, i, 0))
    grp_seq = pl.BlockSpec((1, 1, seq, gw), lambda b, g: (b, g, 0, 0))

    q_scale = QK_HEAD_DIM ** -0.5 * LOG2_E
    ya, gb, qt, k, vt = pl.pallas_call(
        functools.partial(_mixer_in_kernel, d_model=d_model, scale=q_scale),
        grid=(batch, n_tiles),
        in_specs=[tok(d_model),
                  pl.BlockSpec((1, tm, 1), lambda b, i: (b, i, 0)),
                  pl.BlockSpec((1, 1, tm), lambda b, i: (b, 0, i)),
                  _const_spec((1, LANES)), _const_spec((QK_ROPE_DIM // 2, 1)),
                  _const_spec((1, d_model)), _const_spec(win_p.shape),
                  _const_spec((1, a_w)), _const_spec((1, a_w)), _const_spec(a_spatial_w.shape),
                  _const_spec(asb_full.shape), _const_spec((1, Q_LORA_RANK)), _const_spec(wuqt_p.shape),
                  _const_spec((1, KV_LORA_RANK)), _const_spec(wuk_p.shape), _const_spec(wuvt_p.shape)],
        out_specs=[grp_tile, grp_tile,
                   pl.BlockSpec((1, 1, heads, qk_w, tm), lambda b, i: (b, i, 0, 0, 0)),
                   pl.BlockSpec((1, heads, tm, qk_w), lambda b, i: (b, 0, i, 0)),
                   pl.BlockSpec((1, 1, heads * V_HEAD_DIM, tm), lambda b, i: (b, i, 0, 0))],
        out_shape=[grp_arr, grp_arr,
                   jax.ShapeDtypeStruct((batch, n_tiles, heads, qk_w, tm), BF16),
                   jax.ShapeDtypeStruct((batch, heads, seq, qk_w), BF16),
                   jax.ShapeDtypeStruct((batch, n_tiles, heads * V_HEAD_DIM, tm), BF16)],
        compiler_params=_params(2),
        name="mixer_in",
    )(x, pos3, pos_row, inv_freq, inv_freq_col, row(mix_norm), win_p, row(a_v_norm_g), row(a_v_norm_b),
      a_spatial_w, asb_full, row(q_a_norm), wuqt_p, row(kv_a_norm), wuk_p, wuvt_p)

    merged = pl.pallas_call(
        functools.partial(_attention_kernel, blk=tm),
        grid=(batch, n_grp),
        in_specs=[pl.BlockSpec((1, n_tiles, hb, qk_w, tm), lambda b, g: (b, 0, g, 0, 0)),
                  pl.BlockSpec((1, hb, seq, qk_w), lambda b, g: (b, g, 0, 0)),
                  pl.BlockSpec((1, n_tiles, gw, tm), lambda b, g: (b, 0, g, 0)),
                  grp_seq, grp_seq],
        out_specs=grp_seq,
        out_shape=grp_arr,
        compiler_params=_params(2),
        name="attention",
    )(qt, k, vt, ya, gb)

    out = pl.pallas_call(
        functools.partial(_ffn_kernel, d_ff=d_ff, apply_final=apply_final),
        grid=(batch, n_tiles),
        in_specs=[tok(d_model), grp_tile,
                  _const_spec((d_model, d_model)), _const_spec((1, d_model)),
                  _const_spec((d_model, 2 * d_ff)), _const_spec((CONV_WIDTH, 2 * d_ff)),
                  _const_spec((1, 2 * d_ff)), _const_spec((d_ff, d_model)), _const_spec((1, d_model))],
        out_specs=tok(d_model),
        out_shape=jax.ShapeDtypeStruct((batch, seq, d_model), F32),
        scratch_shapes=[pltpu.VMEM((SUBLANES + tm, 2 * d_ff), F32), pltpu.VMEM((tm, d_ff), BF16)],
        compiler_params=_params(2),
        name="ffn",
    )(x, merged, w_out.astype(BF16), row(ffn_norm), w_up.astype(BF16), conv_w, row(conv_b),
      w_down.astype(BF16), row(final_norm))
    return out


def kernel(x, positions, mix_norm, w_in, a_v_norm_g, a_v_norm_b, a_spatial_w, a_spatial_b, q_a_norm, w_uq,
           kv_a_norm, w_ukv, w_out, ffn_norm, w_up, conv_w, conv_b, w_down, final_norm):
    depth = w_in.shape[0]
    batch, seq, _ = x.shape
    pos3 = positions.reshape(batch, seq, 1)
    pos_row = positions.reshape(batch, 1, seq)
    inv_freq_col = (1.0 / (ROPE_THETA ** (jnp.arange(0, QK_ROPE_DIM, 2, dtype=F32) / QK_ROPE_DIM))).reshape(-1, 1)
    inv_freq = jnp.tile(inv_freq_col.reshape(-1), LANES // (QK_ROPE_DIM // 2)).reshape(1, LANES)
    for l in range(depth):
        x = _layer(x, pos3, pos_row, inv_freq, inv_freq_col, mix_norm[l], w_in[l], a_v_norm_g[l], a_v_norm_b[l], a_spatial_w[l],
                   a_spatial_b[l], q_a_norm[l], w_uq[l], kv_a_norm[l], w_ukv[l], w_out[l], ffn_norm[l],
                   w_up[l], conv_w[l], conv_b[l], w_down[l], final_norm, apply_final=(l == depth - 1))
    return x
```

```python
import functools

import jax
import jax.numpy as jnp
from jax import lax
from jax.experimental import pallas as pl
from jax.experimental.pallas import tpu as pltpu

EPS = 1e-6
A_GROUPS = 8
A_GROUP_DIM = 128
CHUNK = 128
MLA_HEADS = 8
QK_NOPE_DIM = 128
QK_ROPE_DIM = 64
QK_HEAD_DIM = QK_NOPE_DIM + QK_ROPE_DIM
V_HEAD_DIM = 128
Q_LORA_RANK = 256
KV_LORA_RANK = 128
ROPE_THETA = 10000.0
CONV_WIDTH = 3

LANES = 128
SUBLANES = 8
VMEM_LIMIT_BYTES = 56 * 1024 * 1024

TOKEN_TILE = 512
ATTN_HEADS_PER_STEP = 4
ATTN_LOOKAHEAD = 3
VT_PAD_ROWS = 16
LOG2_E = 1.4426950408889634
FF_CHUNK = 256

F32 = jnp.float32
BF16 = jnp.bfloat16


def _rms(x, g):
    return x * lax.rsqrt(jnp.mean(x * x, axis=-1, keepdims=True) + EPS) * g


def _dot(a, b):
    return jnp.dot(a, b, preferred_element_type=F32)


def _mixer_in_kernel(x_ref, pos_ref, posr_ref, invf_ref, invfc_ref, mixn_ref, win_ref, avg_ref, avb_ref,
                     asw_ref, asb_ref, qan_ref, wuqt_ref, kvn_ref, wuk_ref, wuvt_ref,
                     ya_ref, gb_ref, qt_ref, k_ref, vt_ref, *, d_model, scale):
    tm = x_ref.shape[1]
    a_w = A_GROUPS * A_GROUP_DIM
    o_u, o_v = 0, a_w
    o_cq = 2 * a_w
    o_ckv = o_cq + Q_LORA_RANK
    o_kr = o_ckv + KV_LORA_RANK
    o_ga = o_kr + 2 * QK_ROPE_DIM
    o_gb = o_ga + d_model
    o_end = o_gb + d_model

    x = x_ref[0]
    h = _rms(x, mixn_ref[...]).astype(BF16)

    def proj(lo, hi):
        return _dot(h, win_ref[:, lo:hi])

    z_cq, z_ckv, z_kr = proj(o_cq, o_ckv), proj(o_ckv, o_kr), proj(o_kr, o_ga)
    z_v, z_u, z_ga, z_gb = proj(o_v, o_cq), proj(o_u, o_v), proj(o_ga, o_gb), proj(o_gb, o_end)

    nt = (((1,), (1,)), ((), ()))
    cqn = _rms(z_cq, qan_ref[...]).astype(BF16)
    ckvn = _rms(z_ckv, kvn_ref[...]).astype(BF16)
    q_t = lax.dot_general(wuqt_ref[...], cqn, nt, preferred_element_type=F32)
    k_nope = _dot(ckvn, wuk_ref[...])
    v_t = lax.dot_general(wuvt_ref[...], ckvn, nt, preferred_element_type=F32).astype(BF16)
    ones_rows = (lax.broadcasted_iota(jnp.int32, (VT_PAD_ROWS, tm), 0) == 0).astype(BF16)
    for hd in range(MLA_HEADS):
        vt_ref[0, 0, hd, 0:V_HEAD_DIM, :] = v_t[hd * V_HEAD_DIM:(hd + 1) * V_HEAD_DIM]
        vt_ref[0, 0, hd, V_HEAD_DIM:V_HEAD_DIM + VT_PAD_ROWS, :] = ones_rows

    v_a = jax.nn.gelu(z_v)
    mu = jnp.mean(v_a, axis=-1, keepdims=True)
    var = jnp.mean(jnp.square(v_a - mu), axis=-1, keepdims=True)
    v_ln = (((v_a - mu) * lax.rsqrt(var + EPS)) * avg_ref[...] + avb_ref[...]).astype(BF16)
    pre = jax.nn.gelu(z_u) * jax.nn.sigmoid(z_ga)
    gate_b = jax.nn.sigmoid(z_gb).astype(BF16)
    gw = gb_ref.shape[3]
    for grp in range(gb_ref.shape[1]):
        gb_ref[0, grp] = gate_b[:, grp * gw:(grp + 1) * gw]

    t_idx = lax.broadcasted_iota(jnp.int32, (CHUNK, CHUNK), 0)
    s_idx = lax.broadcasted_iota(jnp.int32, (CHUNK, CHUNK), 1)
    causal = s_idx <= t_idx
    for g in range(A_GROUPS):
        w_g = jnp.where(causal, asw_ref[g], 0.0).astype(BF16)
        cols = slice(g * A_GROUP_DIM, (g + 1) * A_GROUP_DIM)
        for c in range(tm // CHUNK):
            rows = slice(c * CHUNK, (c + 1) * CHUNK)
            mixed = _dot(w_g, v_ln[rows, cols]) + asb_ref[:, cols]
            lo = (g * A_GROUP_DIM) % gw
            ya_ref[0, (g * A_GROUP_DIM) // gw, rows, lo:lo + A_GROUP_DIM] = (pre[rows, cols] * mixed).astype(BF16)

    ang = pos_ref[0].astype(F32) * invf_ref[...]
    cos = jnp.cos(ang)
    sin = jnp.sin(ang)
    lane = lax.broadcasted_iota(jnp.int32, (1, LANES), 1)
    first_half = (lane % QK_ROPE_DIM) < (QK_ROPE_DIM // 2)
    sin_x1 = jnp.where(first_half, -sin, 0.0)
    sin_x2 = jnp.where(first_half, 0.0, sin)
    half = QK_ROPE_DIM // 2

    def rope(blk):
        return (blk * cos + pltpu.roll(blk, LANES - half, 1) * sin_x1
                + pltpu.roll(blk, half, 1) * sin_x2)

    n_nope = MLA_HEADS * QK_NOPE_DIM
    ang_t = invfc_ref[...] * posr_ref[0].astype(F32)
    cos_t = jnp.cos(ang_t)
    sin_t = jnp.sin(ang_t)
    for p in range(MLA_HEADS // 2):
        parts = []
        for o in (0, QK_ROPE_DIM):
            r = n_nope + p * LANES + o
            x1, x2 = q_t[r:r + half], q_t[r + half:r + 2 * half]
            parts += [x1 * cos_t - x2 * sin_t, x1 * sin_t + x2 * cos_t]
        q_rope = (jnp.concatenate(parts, axis=0) * scale).astype(BF16)
        for hd in (2 * p, 2 * p + 1):
            qt_ref[0, 0, hd, 0:LANES, :] = (q_t[hd * QK_NOPE_DIM:(hd + 1) * QK_NOPE_DIM] * scale).astype(BF16)
            qt_ref[0, 0, hd, LANES:2 * LANES, :] = q_rope

    k_rot = rope(z_kr)
    k_rope = [jnp.where(lane < QK_ROPE_DIM, k_rot, 0.0).astype(BF16),
              jnp.where(lane >= QK_ROPE_DIM, k_rot, 0.0).astype(BF16)]
    for hd in range(MLA_HEADS):
        k_ref[0, hd, :, 0:LANES] = k_nope[:, hd * QK_NOPE_DIM:(hd + 1) * QK_NOPE_DIM].astype(BF16)
        k_ref[0, hd, :, LANES:2 * LANES] = k_rope[hd % 2]


def _attention_kernel(qt_ref, k_ref, vt_ref, ya_ref, gb_ref, o_ref, *, blk):
    hb, seq = k_ref.shape[1], k_ref.shape[2]
    sub = blk // 2
    key_i = lax.broadcasted_iota(jnp.int32, (sub, sub), 0)
    qry_i = lax.broadcasted_iota(jnp.int32, (sub, sub), 1)
    tri = key_i <= qry_i

    order = []
    for qi in range(seq // blk):
        order += [(qi, h, kj, (0, blk), 0, False) for kj in range(qi) for h in range(hb)]
        order += [(qi, h, qi, (0, sub), 0, False) for h in range(hb)]
        order += [(qi, h, qi, (sub, blk), sub, True) for h in range(hb)]

    def scores(qi, h, kj, krows, c0, _):
        k_blk = k_ref[0, h, kj * blk + krows[0]:kj * blk + krows[1], :]
        return _dot(k_blk, qt_ref[0, qi, h, :, c0:blk])

    pending = {t: scores(*order[t]) for t in range(min(ATTN_LOOKAHEAD, len(order)))}
    state = {}
    for t, (qi, h, kj, krows, c0, last) in enumerate(order):
        if t + ATTN_LOOKAHEAD < len(order):
            pending[t + ATTN_LOOKAHEAD] = scores(*order[t + ATTN_LOOKAHEAD])
        s = pending.pop(t)
        rows = slice(qi * blk, (qi + 1) * blk)
        vrows = slice(h * V_HEAD_DIM, (h + 1) * V_HEAD_DIM)
        first = kj == 0 and krows[0] == 0
        if kj == qi and c0 == 0:
            s = jnp.concatenate([jnp.where(tri, s[:, :sub], -jnp.inf), s[:, sub:]], axis=1)
        elif kj == qi:
            s = jnp.where(tri, s, -jnp.inf)
        m_blk = jnp.max(s, axis=0, keepdims=True)
        if first:
            m_new = m_blk
        else:
            m, acc = state[h]
            m_old = m[:, c0:]
            m_new = jnp.maximum(m_old, m_blk)
        p = jnp.exp2(s - m_new).astype(BF16)
        acc_new = _dot(vt_ref[0, kj, h, :, krows[0]:krows[1]], p)
        if not first:
            acc_new = jnp.exp2(m_old - m_new) * acc[:, c0:] + acc_new
            if c0:
                m_new = jnp.concatenate([m[:, :c0], m_new], axis=1)
                acc_new = jnp.concatenate([acc[:, :c0], acc_new], axis=1)
        state[h] = (m_new, acc_new)
        if last:
            y_b = (acc_new[:V_HEAD_DIM] / acc_new[V_HEAD_DIM:V_HEAD_DIM + 1]).T
            merged = ya_ref[0, 0, rows, vrows].astype(F32) + gb_ref[0, 0, rows, vrows].astype(F32) * y_b
            o_ref[0, 0, rows, vrows] = merged.astype(BF16)


def _ffn_kernel(x_ref, mg_ref, wout_ref, ffn_ref, wup_ref, cw_ref, cb_ref, wdown_ref, fin_ref,
                o_ref, up_scr, act_scr, *, d_ff, apply_final):
    tm = x_ref.shape[1]
    halo = SUBLANES
    j = pl.program_id(1)

    @pl.when(j == 0)
    def _():
        up_scr[0:halo, :] = jnp.zeros((halo, up_scr.shape[1]), F32)

    @pl.when(j > 0)
    def _():
        up_scr[0:halo, :] = up_scr[tm:tm + halo, :]

    x1 = x_ref[0]
    gw = mg_ref.shape[3]
    for grp in range(mg_ref.shape[1]):
        x1 = x1 + _dot(mg_ref[0, grp], wout_ref[grp * gw:(grp + 1) * gw, :])
    o_ref[0] = x1
    up_scr[halo:halo + tm, :] = _dot(_rms(x1, ffn_ref[...]).astype(BF16), wup_ref[...])

    def conv(lo):
        cols = slice(lo, lo + FF_CHUNK)
        taps = 0.0
        for k in range(CONV_WIDTH):
            r0 = halo - (CONV_WIDTH - 1) + k
            taps = taps + cw_ref[k:k + 1, cols] * up_scr[r0:r0 + tm, cols]
        return cb_ref[:, cols] + taps

    for c in range(0, d_ff, FF_CHUNK):
        act_scr[:, c:c + FF_CHUNK] = (jax.nn.silu(conv(c)) * conv(d_ff + c)).astype(BF16)

    x2 = o_ref[0] + _dot(act_scr[...], wdown_ref[...])
    o_ref[0] = _rms(x2, fin_ref[...]) if apply_final else x2


def _const_spec(shape):
    nd = len(shape)
    return pl.BlockSpec(shape, lambda *_: (0,) * nd, pipeline_mode=pl.Buffered(1))


def _params(n_axes):
    return pltpu.CompilerParams(dimension_semantics=("arbitrary",) * n_axes,
                                vmem_limit_bytes=VMEM_LIMIT_BYTES)


def _layer(x, pos3, pos_row, inv_freq, inv_freq_col, mix_norm, w_in, a_v_norm_g, a_v_norm_b, a_spatial_w, a_spatial_b,
           q_a_norm, w_uq, kv_a_norm, w_ukv, w_out, ffn_norm, w_up, conv_w, conv_b, w_down,
           final_norm, apply_final):
    batch, seq, d_model = x.shape
    heads = MLA_HEADS
    a_w = A_GROUPS * A_GROUP_DIM
    d_ff = w_down.shape[0]
    assert d_model == a_w == heads * V_HEAD_DIM
    assert seq % TOKEN_TILE == 0 and TOKEN_TILE % CHUNK == 0
    assert d_ff % FF_CHUNK == 0
    tm = TOKEN_TILE
    n_tiles = seq // tm

    s_ckv = 2 * a_w + Q_LORA_RANK
    s_kr = s_ckv + KV_LORA_RANK
    s_ga = s_kr + QK_ROPE_DIM
    win_p = jnp.concatenate([w_in[:, :s_ga], w_in[:, s_kr:s_ga], w_in[:, s_ga:]], axis=1).astype(BF16)
    wuq3 = w_uq.reshape(Q_LORA_RANK, heads, QK_HEAD_DIM)
    wuq_p = jnp.concatenate([wuq3[:, :, :QK_NOPE_DIM].reshape(Q_LORA_RANK, heads * QK_NOPE_DIM),
                             wuq3[:, :, QK_NOPE_DIM:].reshape(Q_LORA_RANK, heads * QK_ROPE_DIM)],
                            axis=1).astype(BF16)
    wukv3 = w_ukv.reshape(KV_LORA_RANK, heads, QK_NOPE_DIM + V_HEAD_DIM)
    wuk_p = wukv3[:, :, :QK_NOPE_DIM].reshape(KV_LORA_RANK, heads * QK_NOPE_DIM).astype(BF16)
    wuvt_p = wukv3[:, :, QK_NOPE_DIM:].reshape(KV_LORA_RANK, heads * V_HEAD_DIM).T.astype(BF16)
    asb_full = jnp.repeat(a_spatial_b.T, A_GROUP_DIM, axis=1)
    row = lambda a: a.reshape(1, -1)

    tok = lambda w: pl.BlockSpec((1, tm, w), lambda b, i: (b, i, 0))
    qk_w = 2 * LANES
    vt_rows = V_HEAD_DIM + VT_PAD_ROWS
    wuqt_p = wuq_p.T
    hb = ATTN_HEADS_PER_STEP
    assert heads % hb == 0
    n_grp, gw = heads // hb, hb * V_HEAD_DIM
    grp_arr = jax.ShapeDtypeStruct((batch, n_grp, seq, gw), BF16)
    grp_tile = pl.BlockSpec((1, n_grp, tm, gw), lambda b, i: (b, 0, i, 0))
    grp_seq = pl.BlockSpec((1, 1, seq, gw), lambda b, g: (b, g, 0, 0))

    q_scale = QK_HEAD_DIM ** -0.5 * LOG2_E
    ya, gb, qt, k, vt = pl.pallas_call(
        functools.partial(_mixer_in_kernel, d_model=d_model, scale=q_scale),
        grid=(batch, n_tiles),
        in_specs=[tok(d_model),
                  pl.BlockSpec((1, tm, 1), lambda b, i: (b, i, 0)),
                  pl.BlockSpec((1, 1, tm), lambda b, i: (b, 0, i)),
                  _const_spec((1, LANES)), _const_spec((QK_ROPE_DIM // 2, 1)),
                  _const_spec((1, d_model)), _const_spec(win_p.shape),
                  _const_spec((1, a_w)), _const_spec((1, a_w)), _const_spec(a_spatial_w.shape),
                  _const_spec(asb_full.shape), _const_spec((1, Q_LORA_RANK)), _const_spec(wuqt_p.shape),
                  _const_spec((1, KV_LORA_RANK)), _const_spec(wuk_p.shape), _const_spec(wuvt_p.shape)],
        out_specs=[grp_tile, grp_tile,
                   pl.BlockSpec((1, 1, heads, qk_w, tm), lambda b, i: (b, i, 0, 0, 0)),
                   pl.BlockSpec((1, heads, tm, qk_w), lambda b, i: (b, 0, i, 0)),
                   pl.BlockSpec((1, 1, heads, vt_rows, tm), lambda b, i: (b, i, 0, 0, 0))],
        out_shape=[grp_arr, grp_arr,
                   jax.ShapeDtypeStruct((batch, n_tiles, heads, qk_w, tm), BF16),
                   jax.ShapeDtypeStruct((batch, heads, seq, qk_w), BF16),
                   jax.ShapeDtypeStruct((batch, n_tiles, heads, vt_rows, tm), BF16)],
        compiler_params=_params(2),
        name="mixer_in",
    )(x, pos3, pos_row, inv_freq, inv_freq_col, row(mix_norm), win_p, row(a_v_norm_g), row(a_v_norm_b),
      a_spatial_w, asb_full, row(q_a_norm), wuqt_p, row(kv_a_norm), wuk_p, wuvt_p)

    merged = pl.pallas_call(
        functools.partial(_attention_kernel, blk=tm),
        grid=(batch, n_grp),
        in_specs=[pl.BlockSpec((1, n_tiles, hb, qk_w, tm), lambda b, g: (b, 0, g, 0, 0)),
                  pl.BlockSpec((1, hb, seq, qk_w), lambda b, g: (b, g, 0, 0)),
                  pl.BlockSpec((1, n_tiles, hb, vt_rows, tm), lambda b, g: (b, 0, g, 0, 0)),
                  grp_seq, grp_seq],
        out_specs=grp_seq,
        out_shape=grp_arr,
        compiler_params=_params(2),
        name="attention",
    )(qt, k, vt, ya, gb)

    out = pl.pallas_call(
        functools.partial(_ffn_kernel, d_ff=d_ff, apply_final=apply_final),
        grid=(batch, n_tiles),
        in_specs=[tok(d_model), grp_tile,
                  _const_spec((d_model, d_model)), _const_spec((1, d_model)),
                  _const_spec((d_model, 2 * d_ff)), _const_spec((CONV_WIDTH, 2 * d_ff)),
                  _const_spec((1, 2 * d_ff)), _const_spec((d_ff, d_model)), _const_spec((1, d_model))],
        out_specs=tok(d_model),
        out_shape=jax.ShapeDtypeStruct((batch, seq, d_model), F32),
        scratch_shapes=[pltpu.VMEM((SUBLANES + tm, 2 * d_ff), F32), pltpu.VMEM((tm, d_ff), BF16)],
        compiler_params=_params(2),
        name="ffn",
    )(x, merged, w_out.astype(BF16), row(ffn_norm), w_up.astype(BF16), conv_w, row(conv_b),
      w_down.astype(BF16), row(final_norm))
    return out


def kernel(x, positions, mix_norm, w_in, a_v_norm_g, a_v_norm_b, a_spatial_w, a_spatial_b, q_a_norm, w_uq,
           kv_a_norm, w_ukv, w_out, ffn_norm, w_up, conv_w, conv_b, w_down, final_norm):
    depth = w_in.shape[0]
    batch, seq, _ = x.shape
    pos3 = positions.reshape(batch, seq, 1)
    pos_row = positions.reshape(batch, 1, seq)
    inv_freq_col = (1.0 / (ROPE_THETA ** (jnp.arange(0, QK_ROPE_DIM, 2, dtype=F32) / QK_ROPE_DIM))).reshape(-1, 1)
    inv_freq = jnp.tile(inv_freq_col.reshape(-1), LANES // (QK_ROPE_DIM // 2)).reshape(1, LANES)
    for l in range(depth):
        x = _layer(x, pos3, pos_row, inv_freq, inv_freq_col, mix_norm[l], w_in[l], a_v_norm_g[l], a_v_norm_b[l], a_spatial_w[l],
                   a_spatial_b[l], q_a_norm[l], w_uq[l], kv_a_norm[l], w_ukv[l], w_out[l], ffn_norm[l],
                   w_up[l], conv_w[l], conv_b[l], w_down[l], final_norm, apply_final=(l == depth - 1))
    return x
```

```python
import functools

import jax
import jax.numpy as jnp
from jax import lax
from jax.experimental import pallas as pl
from jax.experimental.pallas import tpu as pltpu

EPS = 1e-6
A_GROUPS = 8
A_GROUP_DIM = 128
CHUNK = 128
MLA_HEADS = 8
QK_NOPE_DIM = 128
QK_ROPE_DIM = 64
QK_HEAD_DIM = QK_NOPE_DIM + QK_ROPE_DIM
V_HEAD_DIM = 128
Q_LORA_RANK = 256
KV_LORA_RANK = 128
ROPE_THETA = 10000.0
CONV_WIDTH = 3

LANES = 128
SUBLANES = 8
VMEM_LIMIT_BYTES = 56 * 1024 * 1024

TOKEN_TILE = 512
ATTN_HEADS_PER_STEP = 4
ATTN_LOOKAHEAD = 3
VT_PAD_ROWS = 16
LOG2_E = 1.4426950408889634
FF_CHUNK = 256

F32 = jnp.float32
BF16 = jnp.bfloat16


def _rms(x, g):
    return x * lax.rsqrt(jnp.mean(x * x, axis=-1, keepdims=True) + EPS) * g


def _dot(a, b):
    return jnp.dot(a, b, preferred_element_type=F32)


def _sigmoid(x):
    return 0.5 * jnp.tanh(0.5 * x) + 0.5


def _mixer_in_kernel(x_ref, posr_ref, invfc_ref, mixn_ref, win_ref, avg_ref, avb_ref,
                     asw_ref, asb_ref, qan_ref, wuqt_ref, kvn_ref, wuk_ref, wuvt_ref,
                     ya_ref, gb_ref, qt_ref, k_ref, vt_ref, *, d_model, scale):
    tm = x_ref.shape[1]
    a_w = A_GROUPS * A_GROUP_DIM
    o_u, o_v = 0, a_w
    o_cq = 2 * a_w
    o_ckv = o_cq + Q_LORA_RANK
    o_kr = o_ckv + KV_LORA_RANK
    o_ga = o_kr + 2 * QK_ROPE_DIM
    o_gb = o_ga + d_model
    o_end = o_gb + d_model

    x = x_ref[0]
    h = _rms(x, mixn_ref[...]).astype(BF16)

    def proj(lo, hi):
        return _dot(h, win_ref[:, lo:hi])

    z_cq, z_ckv, z_kr = proj(o_cq, o_ckv), proj(o_ckv, o_kr), proj(o_kr, o_ga)
    z_v, z_u, z_ga, z_gb = proj(o_v, o_cq), proj(o_u, o_v), proj(o_ga, o_gb), proj(o_gb, o_end)

    nt = (((1,), (1,)), ((), ()))
    cqn = _rms(z_cq, qan_ref[...]).astype(BF16)
    ckvn = _rms(z_ckv, kvn_ref[...]).astype(BF16)
    q_t = lax.dot_general(wuqt_ref[...], cqn, nt, preferred_element_type=F32)
    k_nope = _dot(ckvn, wuk_ref[...])
    v_t = lax.dot_general(wuvt_ref[...], ckvn, nt, preferred_element_type=F32).astype(BF16)
    ones_rows = (lax.broadcasted_iota(jnp.int32, (VT_PAD_ROWS, tm), 0) == 0).astype(BF16)
    for hd in range(MLA_HEADS):
        vt_ref[0, 0, hd, 0:V_HEAD_DIM, :] = v_t[hd * V_HEAD_DIM:(hd + 1) * V_HEAD_DIM]
        vt_ref[0, 0, hd, V_HEAD_DIM:V_HEAD_DIM + VT_PAD_ROWS, :] = ones_rows

    v_a = jax.nn.gelu(z_v)
    mu = jnp.mean(v_a, axis=-1, keepdims=True)
    var = jnp.mean(jnp.square(v_a - mu), axis=-1, keepdims=True)
    v_ln = (((v_a - mu) * lax.rsqrt(var + EPS)) * avg_ref[...] + avb_ref[...]).astype(BF16)
    pre = jax.nn.gelu(z_u) * _sigmoid(z_ga)
    gate_b = _sigmoid(z_gb).astype(BF16)
    gw = gb_ref.shape[3]
    for grp in range(gb_ref.shape[1]):
        gb_ref[0, grp] = gate_b[:, grp * gw:(grp + 1) * gw]

    t_idx = lax.broadcasted_iota(jnp.int32, (CHUNK, CHUNK), 0)
    s_idx = lax.broadcasted_iota(jnp.int32, (CHUNK, CHUNK), 1)
    causal = s_idx <= t_idx
    for g in range(A_GROUPS):
        w_g = jnp.where(causal, asw_ref[g], 0.0).astype(BF16)
        cols = slice(g * A_GROUP_DIM, (g + 1) * A_GROUP_DIM)
        for c in range(tm // CHUNK):
            rows = slice(c * CHUNK, (c + 1) * CHUNK)
            mixed = _dot(w_g, v_ln[rows, cols]) + asb_ref[:, cols]
            lo = (g * A_GROUP_DIM) % gw
            ya_ref[0, (g * A_GROUP_DIM) // gw, rows, lo:lo + A_GROUP_DIM] = (pre[rows, cols] * mixed).astype(BF16)

    half = QK_ROPE_DIM // 2
    n_nope = MLA_HEADS * QK_NOPE_DIM
    ang_t = invfc_ref[...] * posr_ref[0].astype(F32)
    cos_t = jnp.cos(ang_t)
    sin_t = jnp.sin(ang_t)

    def rope_rows(x1, x2):
        return [x1 * cos_t - x2 * sin_t, x1 * sin_t + x2 * cos_t]

    for p in range(MLA_HEADS // 2):
        parts = []
        for o in (0, QK_ROPE_DIM):
            r = n_nope + p * LANES + o
            parts += rope_rows(q_t[r:r + half], q_t[r + half:r + 2 * half])
        q_rope = (jnp.concatenate(parts, axis=0) * scale).astype(BF16)
        for hd in (2 * p, 2 * p + 1):
            qt_ref[0, 0, hd, 0:LANES, :] = (q_t[hd * QK_NOPE_DIM:(hd + 1) * QK_NOPE_DIM] * scale).astype(BF16)
            qt_ref[0, 0, hd, LANES:2 * LANES, :] = q_rope

    kr_t = z_kr.T
    kr_rot = rope_rows(kr_t[0:half], kr_t[half:2 * half])
    k_rot = jnp.concatenate(kr_rot + kr_rot, axis=0).T
    lane = lax.broadcasted_iota(jnp.int32, (1, LANES), 1)
    k_rope = [jnp.where(lane < QK_ROPE_DIM, k_rot, 0.0).astype(BF16),
              jnp.where(lane >= QK_ROPE_DIM, k_rot, 0.0).astype(BF16)]
    for hd in range(MLA_HEADS):
        k_ref[0, hd, :, 0:LANES] = k_nope[:, hd * QK_NOPE_DIM:(hd + 1) * QK_NOPE_DIM].astype(BF16)
        k_ref[0, hd, :, LANES:2 * LANES] = k_rope[hd % 2]


def _attention_kernel(qt_ref, k_ref, vt_ref, ya_ref, gb_ref, o_ref, *, blk):
    hb, seq = k_ref.shape[1], k_ref.shape[2]
    sub = blk // 2
    key_i = lax.broadcasted_iota(jnp.int32, (sub, sub), 0)
    qry_i = lax.broadcasted_iota(jnp.int32, (sub, sub), 1)
    tri = key_i <= qry_i

    order = []
    for qi in range(seq // blk):
        order += [(qi, h, kj, (0, blk), 0, False) for kj in range(qi) for h in range(hb)]
        order += [(qi, h, qi, (0, sub), 0, False) for h in range(hb)]
        order += [(qi, h, qi, (sub, blk), sub, True) for h in range(hb)]

    def scores(qi, h, kj, krows, c0, _):
        k_blk = k_ref[0, h, kj * blk + krows[0]:kj * blk + krows[1], :]
        return _dot(k_blk, qt_ref[0, qi, h, :, c0:blk])

    pending = {t: scores(*order[t]) for t in range(min(ATTN_LOOKAHEAD, len(order)))}
    state = {}
    for t, (qi, h, kj, krows, c0, last) in enumerate(order):
        if t + ATTN_LOOKAHEAD < len(order):
            pending[t + ATTN_LOOKAHEAD] = scores(*order[t + ATTN_LOOKAHEAD])
        s = pending.pop(t)
        rows = slice(qi * blk, (qi + 1) * blk)
        vrows = slice(h * V_HEAD_DIM, (h + 1) * V_HEAD_DIM)
        first = kj == 0 and krows[0] == 0
        if kj == qi and c0 == 0:
            s = jnp.concatenate([jnp.where(tri, s[:, :sub], -jnp.inf), s[:, sub:]], axis=1)
        elif kj == qi:
            s = jnp.where(tri, s, -jnp.inf)
        m_blk = jnp.max(s, axis=0, keepdims=True)
        if first:
            m_new = m_blk
        else:
            m, acc = state[h]
            m_old = m[:, c0:]
            m_new = jnp.maximum(m_old, m_blk)
        p = jnp.exp2(s - m_new).astype(BF16)
        acc_new = _dot(vt_ref[0, kj, h, :, krows[0]:krows[1]], p)
        if not first:
            acc_new = jnp.exp2(m_old - m_new) * acc[:, c0:] + acc_new
            if c0:
                m_new = jnp.concatenate([m[:, :c0], m_new], axis=1)
                acc_new = jnp.concatenate([acc[:, :c0], acc_new], axis=1)
        state[h] = (m_new, acc_new)
        if last:
            y_b = (acc_new[:V_HEAD_DIM] / acc_new[V_HEAD_DIM:V_HEAD_DIM + 1]).T
            merged = ya_ref[0, 0, rows, vrows].astype(F32) + gb_ref[0, 0, rows, vrows].astype(F32) * y_b
            o_ref[0, 0, rows, vrows] = merged.astype(BF16)


def _ffn_kernel(x_ref, mg_ref, wout_ref, ffn_ref, wup_ref, cw_ref, cb_ref, wdown_ref, fin_ref,
                o_ref, up_scr, act_scr, *, d_ff, apply_final):
    tm = x_ref.shape[1]
    halo = SUBLANES
    j = pl.program_id(1)

    @pl.when(j == 0)
    def _():
        up_scr[0:halo, :] = jnp.zeros((halo, up_scr.shape[1]), F32)

    @pl.when(j > 0)
    def _():
        up_scr[0:halo, :] = up_scr[tm:tm + halo, :]

    x1 = x_ref[0]
    gw = mg_ref.shape[3]
    for grp in range(mg_ref.shape[1]):
        x1 = x1 + _dot(mg_ref[0, grp], wout_ref[grp * gw:(grp + 1) * gw, :])
    o_ref[0] = x1
    up_scr[halo:halo + tm, :] = _dot(_rms(x1, ffn_ref[...]).astype(BF16), wup_ref[...])

    def conv(lo):
        cols = slice(lo, lo + FF_CHUNK)
        taps = 0.0
        for k in range(CONV_WIDTH):
            r0 = halo - (CONV_WIDTH - 1) + k
            taps = taps + cw_ref[k:k + 1, cols] * up_scr[r0:r0 + tm, cols]
        return cb_ref[:, cols] + taps

    for c in range(0, d_ff, FF_CHUNK):
        gate = conv(c)
        act_scr[:, c:c + FF_CHUNK] = (gate * _sigmoid(gate) * conv(d_ff + c)).astype(BF16)

    x2 = o_ref[0] + _dot(act_scr[...], wdown_ref[...])
    o_ref[0] = _rms(x2, fin_ref[...]) if apply_final else x2


def _const_spec(shape):
    nd = len(shape)
    return pl.BlockSpec(shape, lambda *_: (0,) * nd, pipeline_mode=pl.Buffered(1))


def _params(n_axes):
    return pltpu.CompilerParams(dimension_semantics=("arbitrary",) * n_axes,
                                vmem_limit_bytes=VMEM_LIMIT_BYTES)


def _layer(x, pos_row, inv_freq_col, mix_norm, w_in, a_v_norm_g, a_v_norm_b, a_spatial_w, a_spatial_b,
           q_a_norm, w_uq, kv_a_norm, w_ukv, w_out, ffn_norm, w_up, conv_w, conv_b, w_down,
           final_norm, apply_final):
    batch, seq, d_model = x.shape
    heads = MLA_HEADS
    a_w = A_GROUPS * A_GROUP_DIM
    d_ff = w_down.shape[0]
    assert d_model == a_w == heads * V_HEAD_DIM
    assert seq % TOKEN_TILE == 0 and TOKEN_TILE % CHUNK == 0
    assert d_ff % FF_CHUNK == 0
    tm = TOKEN_TILE
    n_tiles = seq // tm

    s_ckv = 2 * a_w + Q_LORA_RANK
    s_kr = s_ckv + KV_LORA_RANK
    s_ga = s_kr + QK_ROPE_DIM
    win_p = jnp.concatenate([w_in[:, :s_ga], w_in[:, s_kr:s_ga], w_in[:, s_ga:]], axis=1).astype(BF16)
    wuq3 = w_uq.reshape(Q_LORA_RANK, heads, QK_HEAD_DIM)
    wuq_p = jnp.concatenate([wuq3[:, :, :QK_NOPE_DIM].reshape(Q_LORA_RANK, heads * QK_NOPE_DIM),
                             wuq3[:, :, QK_NOPE_DIM:].reshape(Q_LORA_RANK, heads * QK_ROPE_DIM)],
                            axis=1).astype(BF16)
    wukv3 = w_ukv.reshape(KV_LORA_RANK, heads, QK_NOPE_DIM + V_HEAD_DIM)
    wuk_p = wukv3[:, :, :QK_NOPE_DIM].reshape(KV_LORA_RANK, heads * QK_NOPE_DIM).astype(BF16)
    wuvt_p = wukv3[:, :, QK_NOPE_DIM:].reshape(KV_LORA_RANK, heads * V_HEAD_DIM).T.astype(BF16)
    asb_full = jnp.repeat(a_spatial_b.T, A_GROUP_DIM, axis=1)
    row = lambda a: a.reshape(1, -1)

    tok = lambda w: pl.BlockSpec((1, tm, w), lambda b, i: (b, i, 0))
    qk_w = 2 * LANES
    vt_rows = V_HEAD_DIM + VT_PAD_ROWS
    wuqt_p = wuq_p.T
    hb = ATTN_HEADS_PER_STEP
    assert heads % hb == 0
    n_grp, gw = heads // hb, hb * V_HEAD_DIM
    grp_arr = jax.ShapeDtypeStruct((batch, n_grp, seq, gw), BF16)
    grp_tile = pl.BlockSpec((1, n_grp, tm, gw), lambda b, i: (b, 0, i, 0))
    grp_seq = pl.BlockSpec((1, 1, seq, gw), lambda b, g: (b, g, 0, 0))

    q_scale = QK_HEAD_DIM ** -0.5 * LOG2_E
    ya, gb, qt, k, vt = pl.pallas_call(
        functools.partial(_mixer_in_kernel, d_model=d_model, scale=q_scale),
        grid=(batch, n_tiles),
        in_specs=[tok(d_model),
                  pl.BlockSpec((1, 1, tm), lambda b, i: (b, 0, i)),
                  _const_spec((QK_ROPE_DIM // 2, 1)),
                  _const_spec((1, d_model)), _const_spec(win_p.shape),
                  _const_spec((1, a_w)), _const_spec((1, a_w)), _const_spec(a_spatial_w.shape),
                  _const_spec(asb_full.shape), _const_spec((1, Q_LORA_RANK)), _const_spec(wuqt_p.shape),
                  _const_spec((1, KV_LORA_RANK)), _const_spec(wuk_p.shape), _const_spec(wuvt_p.shape)],
        out_specs=[grp_tile, grp_tile,
                   pl.BlockSpec((1, 1, heads, qk_w, tm), lambda b, i: (b, i, 0, 0, 0)),
                   pl.BlockSpec((1, heads, tm, qk_w), lambda b, i: (b, 0, i, 0)),
                   pl.BlockSpec((1, 1, heads, vt_rows, tm), lambda b, i: (b, i, 0, 0, 0))],
        out_shape=[grp_arr, grp_arr,
                   jax.ShapeDtypeStruct((batch, n_tiles, heads, qk_w, tm), BF16),
                   jax.ShapeDtypeStruct((batch, heads, seq, qk_w), BF16),
                   jax.ShapeDtypeStruct((batch, n_tiles, heads, vt_rows, tm), BF16)],
        compiler_params=_params(2),
        name="mixer_in",
    )(x, pos_row, inv_freq_col, row(mix_norm), win_p, row(a_v_norm_g), row(a_v_norm_b),
      a_spatial_w, asb_full, row(q_a_norm), wuqt_p, row(kv_a_norm), wuk_p, wuvt_p)

    merged = pl.pallas_call(
        functools.partial(_attention_kernel, blk=tm),
        grid=(batch, n_grp),
        in_specs=[pl.BlockSpec((1, n_tiles, hb, qk_w, tm), lambda b, g: (b, 0, g, 0, 0)),
                  pl.BlockSpec((1, hb, seq, qk_w), lambda b, g: (b, g, 0, 0)),
                  pl.BlockSpec((1, n_tiles, hb, vt_rows, tm), lambda b, g: (b, 0, g, 0, 0)),
                  grp_seq, grp_seq],
        out_specs=grp_seq,
        out_shape=grp_arr,
        compiler_params=_params(2),
        name="attention",
    )(qt, k, vt, ya, gb)

    out = pl.pallas_call(
        functools.partial(_ffn_kernel, d_ff=d_ff, apply_final=apply_final),
        grid=(batch, n_tiles),
        in_specs=[tok(d_model), grp_tile,
                  _const_spec((d_model, d_model)), _const_spec((1, d_model)),
                  _const_spec((d_model, 2 * d_ff)), _const_spec((CONV_WIDTH, 2 * d_ff)),
                  _const_spec((1, 2 * d_ff)), _const_spec((d_ff, d_model)), _const_spec((1, d_model))],
        out_specs=tok(d_model),
        out_shape=jax.ShapeDtypeStruct((batch, seq, d_model), F32),
        scratch_shapes=[pltpu.VMEM((SUBLANES + tm, 2 * d_ff), F32), pltpu.VMEM((tm, d_ff), BF16)],
        compiler_params=_params(2),
        name="ffn",
    )(x, merged, w_out.astype(BF16), row(ffn_norm), w_up.astype(BF16), conv_w, row(conv_b),
      w_down.astype(BF16), row(final_norm))
    return out


def kernel(x, positions, mix_norm, w_in, a_v_norm_g, a_v_norm_b, a_spatial_w, a_spatial_b, q_a_norm, w_uq,
           kv_a_norm, w_ukv, w_out, ffn_norm, w_up, conv_w, conv_b, w_down, final_norm):
    depth = w_in.shape[0]
    batch, seq, _ = x.shape
    pos_row = positions.reshape(batch, 1, seq)
    inv_freq_col = (1.0 / (ROPE_THETA ** (jnp.arange(0, QK_ROPE_DIM, 2, dtype=F32) / QK_ROPE_DIM))).reshape(-1, 1)
    for l in range(depth):
        x = _layer(x, pos_row, inv_freq_col, mix_norm[l], w_in[l], a_v_norm_g[l], a_v_norm_b[l], a_spatial_w[l],
                   a_spatial_b[l], q_a_norm[l], w_uq[l], kv_a_norm[l], w_ukv[l], w_out[l], ffn_norm[l],
                   w_up[l], conv_w[l], conv_b[l], w_down[l], final_norm, apply_final=(l == depth - 1))
    return x
```

```python
import functools

import jax
import jax.numpy as jnp
from jax import lax
from jax.experimental import pallas as pl
from jax.experimental.pallas import tpu as pltpu

EPS = 1e-6
A_GROUPS = 8
A_GROUP_DIM = 128
CHUNK = 128
MLA_HEADS = 8
QK_NOPE_DIM = 128
QK_ROPE_DIM = 64
QK_HEAD_DIM = QK_NOPE_DIM + QK_ROPE_DIM
V_HEAD_DIM = 128
Q_LORA_RANK = 256
KV_LORA_RANK = 128
ROPE_THETA = 10000.0
CONV_WIDTH = 3

LANES = 128
SUBLANES = 8
VMEM_LIMIT_BYTES = 56 * 1024 * 1024

TOKEN_TILE = 512
ATTN_HEADS_PER_STEP = 4
ATTN_LOOKAHEAD = 2
VT_PAD_ROWS = 16
LOG2_E = 1.4426950408889634
FF_CHUNK = 256

F32 = jnp.float32
BF16 = jnp.bfloat16


def _rms(x, g):
    return x * lax.rsqrt(jnp.mean(x * x, axis=-1, keepdims=True) + EPS) * g


def _dot(a, b):
    return jnp.dot(a, b, preferred_element_type=F32)


def _sigmoid(x):
    return 0.5 * jnp.tanh(0.5 * x) + 0.5


def _mixer_in_kernel(x_ref, posr_ref, invfc_ref, mixn_ref, wmix_ref, wgate_ref, avg_ref, avb_ref,
                     asw_ref, asb_ref, qan_ref, wuqt_ref, kvn_ref, wuk_ref, wuvt_ref,
                     ya_ref, gb_ref, qt_ref, k_ref, vt_ref, *, d_model, scale):
    tm = x_ref.shape[1]
    a_w = A_GROUPS * A_GROUP_DIM
    o_cq = 2 * a_w
    o_ckv = o_cq + Q_LORA_RANK

    x = x_ref[0]
    h = _rms(x, mixn_ref[...]).astype(BF16)

    z_cq = _dot(h, wmix_ref[:, o_cq:o_ckv])
    z_ckv_kr = _dot(h, wmix_ref[:, o_ckv:o_ckv + KV_LORA_RANK + LANES])
    z_ckv = z_ckv_kr[:, :KV_LORA_RANK]
    z_kr = z_ckv_kr[:, KV_LORA_RANK:]
    z_v, z_u = _dot(h, wmix_ref[:, a_w:o_cq]), _dot(h, wmix_ref[:, 0:a_w])
    z_ga, z_gb = _dot(h, wgate_ref[:, 0:d_model]), _dot(h, wgate_ref[:, d_model:2 * d_model])

    nt = (((1,), (1,)), ((), ()))
    cqn = _rms(z_cq, qan_ref[...]).astype(BF16)
    ckvn = _rms(z_ckv, kvn_ref[...]).astype(BF16)
    q_t = lax.dot_general(wuqt_ref[...], cqn, nt, preferred_element_type=F32)
    k_nope = _dot(ckvn, wuk_ref[...])
    v_t = lax.dot_general(wuvt_ref[...], ckvn, nt, preferred_element_type=F32).astype(BF16)
    ones_rows = (lax.broadcasted_iota(jnp.int32, (VT_PAD_ROWS, tm), 0) == 0).astype(BF16)
    for hd in range(MLA_HEADS):
        vt_ref[0, 0, hd, 0:V_HEAD_DIM, :] = v_t[hd * V_HEAD_DIM:(hd + 1) * V_HEAD_DIM]
        vt_ref[0, 0, hd, V_HEAD_DIM:V_HEAD_DIM + VT_PAD_ROWS, :] = ones_rows

    v_a = jax.nn.gelu(z_v)
    mu = jnp.mean(v_a, axis=-1, keepdims=True)
    var = jnp.mean(jnp.square(v_a - mu), axis=-1, keepdims=True)
    v_ln = (((v_a - mu) * lax.rsqrt(var + EPS)) * avg_ref[...] + avb_ref[...]).astype(BF16)
    pre = jax.nn.gelu(z_u) * _sigmoid(z_ga)
    gate_b = _sigmoid(z_gb).astype(BF16)
    gw = gb_ref.shape[3]
    for grp in range(gb_ref.shape[1]):
        gb_ref[0, grp] = gate_b[:, grp * gw:(grp + 1) * gw]

    t_idx = lax.broadcasted_iota(jnp.int32, (CHUNK, CHUNK), 0)
    s_idx = lax.broadcasted_iota(jnp.int32, (CHUNK, CHUNK), 1)
    causal = s_idx <= t_idx
    for g in range(A_GROUPS):
        w_g = jnp.where(causal, asw_ref[g], 0.0).astype(BF16)
        cols = slice(g * A_GROUP_DIM, (g + 1) * A_GROUP_DIM)
        for c in range(tm // CHUNK):
            rows = slice(c * CHUNK, (c + 1) * CHUNK)
            mixed = _dot(w_g, v_ln[rows, cols]) + asb_ref[:, cols]
            lo = (g * A_GROUP_DIM) % gw
            ya_ref[0, (g * A_GROUP_DIM) // gw, rows, lo:lo + A_GROUP_DIM] = (pre[rows, cols] * mixed).astype(BF16)

    half = QK_ROPE_DIM // 2
    n_nope = MLA_HEADS * QK_NOPE_DIM
    ang_t = invfc_ref[...] * posr_ref[0].astype(F32)
    cos_t = jnp.cos(ang_t)
    sin_t = jnp.sin(ang_t)

    def rope_rows(x1, x2):
        return [x1 * cos_t - x2 * sin_t, x1 * sin_t + x2 * cos_t]

    for p in range(MLA_HEADS // 2):
        parts = []
        for o in (0, QK_ROPE_DIM):
            r = n_nope + p * LANES + o
            parts += rope_rows(q_t[r:r + half], q_t[r + half:r + 2 * half])
        q_rope = (jnp.concatenate(parts, axis=0) * scale).astype(BF16)
        for hd in (2 * p, 2 * p + 1):
            qt_ref[0, 0, hd, 0:LANES, :] = (q_t[hd * QK_NOPE_DIM:(hd + 1) * QK_NOPE_DIM] * scale).astype(BF16)
            qt_ref[0, 0, hd, LANES:2 * LANES, :] = q_rope

    kr_t = z_kr.T
    kr_rot = rope_rows(kr_t[0:half], kr_t[half:2 * half])
    k_rot = jnp.concatenate(kr_rot + kr_rot, axis=0).T
    lane = lax.broadcasted_iota(jnp.int32, (1, LANES), 1)
    k_rope = [jnp.where(lane < QK_ROPE_DIM, k_rot, 0.0).astype(BF16),
              jnp.where(lane >= QK_ROPE_DIM, k_rot, 0.0).astype(BF16)]
    for hd in range(MLA_HEADS):
        k_ref[0, hd, :, 0:LANES] = k_nope[:, hd * QK_NOPE_DIM:(hd + 1) * QK_NOPE_DIM].astype(BF16)
        k_ref[0, hd, :, LANES:2 * LANES] = k_rope[hd % 2]


def _attention_kernel(qt_ref, k_ref, vt_ref, ya_ref, gb_ref, o_ref, *, blk):
    hb, seq = k_ref.shape[1], k_ref.shape[2]
    sub = blk // 2
    key_i = lax.broadcasted_iota(jnp.int32, (sub, sub), 0)
    qry_i = lax.broadcasted_iota(jnp.int32, (sub, sub), 1)
    tri = key_i <= qry_i

    order = []
    for qi in range(seq // blk):
        order += [(qi, h, kj, (0, blk), 0, False) for kj in range(qi) for h in range(hb)]
        order += [(qi, h, qi, (0, sub), 0, False) for h in range(hb)]
        order += [(qi, h, qi, (sub, blk), sub, True) for h in range(hb)]

    def scores(qi, h, kj, krows, c0, _):
        k_blk = k_ref[0, h, kj * blk + krows[0]:kj * blk + krows[1], :]
        return _dot(k_blk, qt_ref[0, qi, h, :, c0:blk])

    pending = {t: scores(*order[t]) for t in range(min(ATTN_LOOKAHEAD, len(order)))}
    state = {}
    for t, (qi, h, kj, krows, c0, last) in enumerate(order):
        if t + ATTN_LOOKAHEAD < len(order):
            pending[t + ATTN_LOOKAHEAD] = scores(*order[t + ATTN_LOOKAHEAD])
        s = pending.pop(t)
        rows = slice(qi * blk, (qi + 1) * blk)
        vrows = slice(h * V_HEAD_DIM, (h + 1) * V_HEAD_DIM)
        first = kj == 0 and krows[0] == 0
        if kj == qi and c0 == 0:
            s = jnp.concatenate([jnp.where(tri, s[:, :sub], -jnp.inf), s[:, sub:]], axis=1)
        elif kj == qi:
            s = jnp.where(tri, s, -jnp.inf)
        m_blk = jnp.max(s, axis=0, keepdims=True)
        if first:
            m_new = m_blk
        else:
            m, acc = state[h]
            m_old = m[:, c0:]
            m_new = jnp.maximum(m_old, m_blk)
        p = jnp.exp2(s - m_new).astype(BF16)
        acc_new = _dot(vt_ref[0, kj, h, :, krows[0]:krows[1]], p)
        if not first:
            acc_new = jnp.exp2(m_old - m_new) * acc[:, c0:] + acc_new
            if c0:
                m_new = jnp.concatenate([m[:, :c0], m_new], axis=1)
                acc_new = jnp.concatenate([acc[:, :c0], acc_new], axis=1)
        state[h] = (m_new, acc_new)
        if last:
            y_b = (acc_new[:V_HEAD_DIM] / acc_new[V_HEAD_DIM:V_HEAD_DIM + 1]).T
            merged = ya_ref[0, 0, rows, vrows].astype(F32) + gb_ref[0, 0, rows, vrows].astype(F32) * y_b
            o_ref[0, 0, rows, vrows] = merged.astype(BF16)


def _ffn_kernel(x_ref, mg_ref, wout_ref, ffn_ref, wup_ref, cw_ref, cb_ref, wdown_ref, fin_ref,
                o_ref, up_scr, act_scr, *, d_ff, apply_final):
    tm = x_ref.shape[1]
    halo = SUBLANES
    j = pl.program_id(1)

    @pl.when(j == 0)
    def _():
        up_scr[0:halo, :] = jnp.zeros((halo, up_scr.shape[1]), F32)

    @pl.when(j > 0)
    def _():
        up_scr[0:halo, :] = up_scr[tm:tm + halo, :]

    x1 = x_ref[0]
    gw = mg_ref.shape[3]
    for grp in range(mg_ref.shape[1]):
        x1 = x1 + _dot(mg_ref[0, grp], wout_ref[grp * gw:(grp + 1) * gw, :])
    o_ref[0] = x1
    up_scr[halo:halo + tm, :] = _dot(_rms(x1, ffn_ref[...]).astype(BF16), wup_ref[...])

    def conv(lo, scale=1.0):
        cols = slice(lo, lo + FF_CHUNK)
        taps = 0.0
        for k in range(CONV_WIDTH):
            r0 = halo - (CONV_WIDTH - 1) + k
            taps = taps + (cw_ref[k:k + 1, cols] * scale) * up_scr[r0:r0 + tm, cols]
        return cb_ref[:, cols] * scale + taps

    for c in range(0, d_ff, FF_CHUNK):
        hg = conv(c, 0.5)
        act_scr[:, c:c + FF_CHUNK] = ((hg * jnp.tanh(hg) + hg) * conv(d_ff + c)).astype(BF16)

    x2 = o_ref[0] + _dot(act_scr[...], wdown_ref[...])
    o_ref[0] = _rms(x2, fin_ref[...]) if apply_final else x2


def _const_spec(shape):
    nd = len(shape)
    return pl.BlockSpec(shape, lambda *_: (0,) * nd, pipeline_mode=pl.Buffered(1))


def _params(n_axes):
    return pltpu.CompilerParams(dimension_semantics=("arbitrary",) * n_axes,
                                vmem_limit_bytes=VMEM_LIMIT_BYTES)


def _layer(x, pos_row, inv_freq_col, mix_norm, w_in, a_v_norm_g, a_v_norm_b, a_spatial_w, a_spatial_b,
           q_a_norm, w_uq, kv_a_norm, w_ukv, w_out, ffn_norm, w_up, conv_w, conv_b, w_down,
           final_norm, apply_final):
    batch, seq, d_model = x.shape
    heads = MLA_HEADS
    a_w = A_GROUPS * A_GROUP_DIM
    d_ff = w_down.shape[0]
    assert d_model == a_w == heads * V_HEAD_DIM
    assert seq % TOKEN_TILE == 0 and TOKEN_TILE % CHUNK == 0
    assert d_ff % FF_CHUNK == 0
    tm = TOKEN_TILE
    n_tiles = seq // tm

    s_ckv = 2 * a_w + Q_LORA_RANK
    s_kr = s_ckv + KV_LORA_RANK
    s_ga = s_kr + QK_ROPE_DIM
    wmix_p = w_in[:, :s_kr + LANES].astype(BF16)
    wgate_p = w_in[:, s_ga:].astype(BF16)
    wuq3 = w_uq.reshape(Q_LORA_RANK, heads, QK_HEAD_DIM)
    wuq_p = jnp.concatenate([wuq3[:, :, :QK_NOPE_DIM].reshape(Q_LORA_RANK, heads * QK_NOPE_DIM),
                             wuq3[:, :, QK_NOPE_DIM:].reshape(Q_LORA_RANK, heads * QK_ROPE_DIM)],
                            axis=1).astype(BF16)
    wukv3 = w_ukv.reshape(KV_LORA_RANK, heads, QK_NOPE_DIM + V_HEAD_DIM)
    wuk_p = wukv3[:, :, :QK_NOPE_DIM].reshape(KV_LORA_RANK, heads * QK_NOPE_DIM).astype(BF16)
    wuvt_p = wukv3[:, :, QK_NOPE_DIM:].reshape(KV_LORA_RANK, heads * V_HEAD_DIM).T.astype(BF16)
    asb_full = jnp.repeat(a_spatial_b.T, A_GROUP_DIM, axis=1)
    row = lambda a: a.reshape(1, -1)

    tok = lambda w: pl.BlockSpec((1, tm, w), lambda b, i: (b, i, 0))
    qk_w = 2 * LANES
    vt_rows = V_HEAD_DIM + VT_PAD_ROWS
    wuqt_p = wuq_p.T
    hb = ATTN_HEADS_PER_STEP
    assert heads % hb == 0
    n_grp, gw = heads // hb, hb * V_HEAD_DIM
    grp_arr = jax.ShapeDtypeStruct((batch, n_grp, seq, gw), BF16)
    grp_tile = pl.BlockSpec((1, n_grp, tm, gw), lambda b, i: (b, 0, i, 0))
    grp_seq = pl.BlockSpec((1, 1, seq, gw), lambda b, g: (b, g, 0, 0))

    q_scale = QK_HEAD_DIM ** -0.5 * LOG2_E
    ya, gb, qt, k, vt = pl.pallas_call(
        functools.partial(_mixer_in_kernel, d_model=d_model, scale=q_scale),
        grid=(batch, n_tiles),
        in_specs=[tok(d_model),
                  pl.BlockSpec((1, 1, tm), lambda b, i: (b, 0, i)),
                  _const_spec((QK_ROPE_DIM // 2, 1)),
                  _const_spec((1, d_model)),
                  _const_spec(wmix_p.shape), _const_spec(wgate_p.shape),
                  _const_spec((1, a_w)), _const_spec((1, a_w)), _const_spec(a_spatial_w.shape),
                  _const_spec(asb_full.shape), _const_spec((1, Q_LORA_RANK)), _const_spec(wuqt_p.shape),
                  _const_spec((1, KV_LORA_RANK)), _const_spec(wuk_p.shape), _const_spec(wuvt_p.shape)],
        out_specs=[grp_tile, grp_tile,
                   pl.BlockSpec((1, 1, heads, qk_w, tm), lambda b, i: (b, i, 0, 0, 0)),
                   pl.BlockSpec((1, heads, tm, qk_w), lambda b, i: (b, 0, i, 0)),
                   pl.BlockSpec((1, 1, heads, vt_rows, tm), lambda b, i: (b, i, 0, 0, 0))],
        out_shape=[grp_arr, grp_arr,
                   jax.ShapeDtypeStruct((batch, n_tiles, heads, qk_w, tm), BF16),
                   jax.ShapeDtypeStruct((batch, heads, seq, qk_w), BF16),
                   jax.ShapeDtypeStruct((batch, n_tiles, heads, vt_rows, tm), BF16)],
        compiler_params=_params(2),
        name="mixer_in",
    )(x, pos_row, inv_freq_col, row(mix_norm), wmix_p, wgate_p, row(a_v_norm_g), row(a_v_norm_b),
      a_spatial_w, asb_full, row(q_a_norm), wuqt_p, row(kv_a_norm), wuk_p, wuvt_p)

    merged = pl.pallas_call(
        functools.partial(_attention_kernel, blk=tm),
        grid=(batch, n_grp),
        in_specs=[pl.BlockSpec((1, n_tiles, hb, qk_w, tm), lambda b, g: (b, 0, g, 0, 0)),
                  pl.BlockSpec((1, hb, seq, qk_w), lambda b, g: (b, g, 0, 0)),
                  pl.BlockSpec((1, n_tiles, hb, vt_rows, tm), lambda b, g: (b, 0, g, 0, 0)),
                  grp_seq, grp_seq],
        out_specs=grp_seq,
        out_shape=grp_arr,
        compiler_params=_params(2),
        name="attention",
    )(qt, k, vt, ya, gb)

    out = pl.pallas_call(
        functools.partial(_ffn_kernel, d_ff=d_ff, apply_final=apply_final),
        grid=(batch, n_tiles),
        in_specs=[tok(d_model), grp_tile,
                  _const_spec((d_model, d_model)), _const_spec((1, d_model)),
                  _const_spec((d_model, 2 * d_ff)), _const_spec((CONV_WIDTH, 2 * d_ff)),
                  _const_spec((1, 2 * d_ff)), _const_spec((d_ff, d_model)), _const_spec((1, d_model))],
        out_specs=tok(d_model),
        out_shape=jax.ShapeDtypeStruct((batch, seq, d_model), F32),
        scratch_shapes=[pltpu.VMEM((SUBLANES + tm, 2 * d_ff), F32), pltpu.VMEM((tm, d_ff), BF16)],
        compiler_params=_params(2),
        name="ffn",
    )(x, merged, w_out.astype(BF16), row(ffn_norm), w_up.astype(BF16), conv_w, row(conv_b),
      w_down.astype(BF16), row(final_norm))
    return out


def kernel(x, positions, mix_norm, w_in, a_v_norm_g, a_v_norm_b, a_spatial_w, a_spatial_b, q_a_norm, w_uq,
           kv_a_norm, w_ukv, w_out, ffn_norm, w_up, conv_w, conv_b, w_down, final_norm):
    depth = w_in.shape[0]
    batch, seq, _ = x.shape
    pos_row = positions.reshape(batch, 1, seq)
    inv_freq_col = (1.0 / (ROPE_THETA ** (jnp.arange(0, QK_ROPE_DIM, 2, dtype=F32) / QK_ROPE_DIM))).reshape(-1, 1)
    for l in range(depth):
        x = _layer(x, pos_row, inv_freq_col, mix_norm[l], w_in[l], a_v_norm_g[l], a_v_norm_b[l], a_spatial_w[l],
                   a_spatial_b[l], q_a_norm[l], w_uq[l], kv_a_norm[l], w_ukv[l], w_out[l], ffn_norm[l],
                   w_up[l], conv_w[l], conv_b[l], w_down[l], final_norm, apply_final=(l == depth - 1))
    return x
```
